```python
import jax, jax.numpy as jnp
from jax import lax
import numpy as np

D_MODEL = 1024
BATCH = 8
SEQ = 2048
DEPTH = 2
DEC_BATCH = 8
DEC_SEQ = 32
PAST_LEN = 1024

CHUNK = 64
MIX_WIDTH = D_MODEL
SGU_WIDTH = MIX_WIDTH // 2
SGU_GROUPS = 4
SGU_GROUP_DIM = SGU_WIDTH // SGU_GROUPS
SGU_CHUNK = 128
DN_WIDTH = MIX_WIDTH - SGU_WIDTH
DN_HEADS = 4
DN_HEAD_DIM = DN_WIDTH // DN_HEADS
CONV_W = 4
CONV_CH = 3 * DN_WIDTH
D_FF = 4 * D_MODEL
IN_WIDTH = 2 * SGU_WIDTH + CONV_CH + DN_WIDTH + 2 * DN_HEADS
EPS = 1e-6

kernel_name = "hybrid_sgu_gdn_stream_step"


def rmsnorm(x, g):
    xf = x.astype(jnp.float32)
    y = xf * lax.rsqrt(jnp.mean(xf * xf, axis=-1, keepdims=True) + EPS)
    return (y * g.astype(jnp.float32)).astype(x.dtype)


def l2norm(x):
    xf = x.astype(jnp.float32)
    return xf * lax.rsqrt(jnp.sum(xf * xf, axis=-1, keepdims=True) + EPS)


def spatial_gating(u, v, w_s, b_s):
    B, L, _ = u.shape
    P = SGU_CHUNK if L % SGU_CHUNK == 0 else L
    n = L // P
    blk = jnp.arange(P) // CHUNK
    mask = blk[None, :] <= blk[:, None]
    w = jnp.where(mask[None], w_s[:, :P, :P], 0).astype(v.dtype)
    vb = v.reshape(B, n, P, SGU_GROUPS, SGU_GROUP_DIM)
    bias = jnp.swapaxes(b_s[:, :P], 0, 1)[None, None, :, :, None]
    s = jnp.einsum('gpq,bnqgc->bnpgc', w, vb) + bias.astype(v.dtype)
    return u * s.reshape(B, L, SGU_WIDTH)


def causal_conv(x, buf, w):
    L = x.shape[1]
    xp = jnp.concatenate([buf.astype(x.dtype), x], axis=1)
    y = xp[:, 0:L] * w[0]
    for i in range(1, CONV_W):
        y = y + xp[:, i:i + L] * w[i]
    return y, xp[:, -(CONV_W - 1):]


def gated_delta_rule(q, k, v, beta, g, s0):
    B, L, H, Dk = q.shape
    Dv = v.shape[-1]
    C = CHUNK if L % CHUNK == 0 else L
    n = L // C

    def blk(t):
        t = t.reshape((B, n, C, H) + t.shape[3:])
        return jnp.moveaxis(t, 3, 1)

    q, k, v, beta, g = blk(q), blk(k), blk(v), blk(beta), blk(g)
    gc = jnp.cumsum(g, axis=-1)
    diff = gc[..., :, None] - gc[..., None, :]
    idx = jnp.arange(C)
    incl = idx[:, None] >= idx[None, :]
    strict = idx[:, None] > idx[None, :]
    dec_incl = jnp.exp(jnp.where(incl, diff, -jnp.inf))
    dec_strict = jnp.where(strict, dec_incl, 0.0)
    kb = k * beta[..., None]
    a = jnp.einsum('bhnid,bhnjd->bhnij', kb, k) * dec_strict
    lmat = a + jnp.eye(C, dtype=a.dtype)
    rhs = jnp.concatenate([v * beta[..., None], kb * jnp.exp(gc)[..., None]], axis=-1)
    sol = lax.linalg.triangular_solve(lmat, rhs, left_side=True, lower=True, unit_diagonal=True)
    u_c, w_c = sol[..., :Dv], sol[..., Dv:]
    attn = jnp.einsum('bhnid,bhnjd->bhnij', q, k) * dec_incl
    qg = q * jnp.exp(gc)[..., None]
    kg = k * jnp.exp(gc[..., -1:] - gc)[..., None]
    glast = jnp.exp(gc[..., -1])
    xs = (jnp.moveaxis(u_c, 2, 0), jnp.moveaxis(w_c, 2, 0), jnp.moveaxis(attn, 2, 0),
          jnp.moveaxis(qg, 2, 0), jnp.moveaxis(kg, 2, 0), jnp.moveaxis(glast, 2, 0))

    def step(s, inp):
        u_i, w_i, attn_i, qg_i, kg_i, gl_i = inp
        v_new = u_i - jnp.einsum('bhck,bhkv->bhcv', w_i, s)
        o = jnp.einsum('bhck,bhkv->bhcv', qg_i, s) + jnp.einsum('bhij,bhjv->bhiv', attn_i, v_new)
        s = s * gl_i[..., None, None] + jnp.einsum('bhck,bhcv->bhkv', kg_i, v_new)
        return s, o

    s_fin, o = lax.scan(step, s0, xs)
    o = jnp.transpose(o, (1, 0, 3, 2, 4)).reshape(B, L, H, Dv)
    return o, s_fin


def mixer(h, conv_buf, s0, w_in, sgu_norm_g, sgu_w, sgu_b, conv_w, dt_bias, a_log, dn_norm_g, w_out):
    B, L, _ = h.shape
    p = h @ w_in
    o1 = 2 * SGU_WIDTH
    o2 = o1 + CONV_CH
    o3 = o2 + DN_WIDTH
    uv, qkv, z, ab = p[..., :o1], p[..., o1:o2], p[..., o2:o3], p[..., o3:]
    uv = jax.nn.gelu(uv)
    u, v = uv[..., :SGU_WIDTH], uv[..., SGU_WIDTH:]
    v = rmsnorm(v, sgu_norm_g)
    a_out = spatial_gating(u, v, sgu_w, sgu_b)
    qkv, new_buf = causal_conv(qkv, conv_buf, conv_w)
    qkv = jax.nn.silu(qkv).reshape(B, L, 3 * DN_HEADS, DN_HEAD_DIM)
    q = l2norm(qkv[:, :, :DN_HEADS]) * (DN_HEAD_DIM ** -0.5)
    k = l2norm(qkv[:, :, DN_HEADS:2 * DN_HEADS])
    vv = qkv[:, :, 2 * DN_HEADS:].astype(jnp.float32)
    abf = ab.astype(jnp.float32)
    beta = jax.nn.sigmoid(abf[..., :DN_HEADS])
    g = -jnp.exp(a_log.astype(jnp.float32)) * jax.nn.softplus(abf[..., DN_HEADS:] + dt_bias.astype(jnp.float32))
    o, s_new = gated_delta_rule(q, k, vv, beta, g, s0.astype(jnp.float32))
    zf = jax.nn.silu(z.astype(jnp.float32)).reshape(B, L, DN_HEADS, DN_HEAD_DIM)
    b_out = (rmsnorm(o, dn_norm_g) * zf).astype(h.dtype).reshape(B, L, DN_WIDTH)
    out = jnp.concatenate([a_out, b_out], axis=-1) @ w_out
    return out, new_buf, s_new, v


def trunk(x, c, conv_state, delta_state, ada_w, ada_b, norm_mix_g, norm_ffn_g, w_in, sgu_norm_g,
          sgu_w, sgu_b, conv_w, dt_bias, a_log, dn_norm_g, w_out, w_up, w_down, final_norm_g):
    convs, deltas, vrows = [], [], []
    cs = jax.nn.silu(c)
    for l in range(DEPTH):
        mod = (cs @ ada_w[l] + ada_b[l])[:, None, :]
        sh1, sc1, gt1, sh2, sc2, gt2 = jnp.split(mod, 6, axis=-1)
        h = rmsnorm(x, norm_mix_g[l]) * (1 + sc1) + sh1
        m, nb, ns, vr = mixer(h, conv_state[l], delta_state[l], w_in[l], sgu_norm_g[l], sgu_w[l], sgu_b[l],
                              conv_w[l], dt_bias[l], a_log[l], dn_norm_g[l], w_out[l])
        x = x + gt1 * m
        h = rmsnorm(x, norm_ffn_g[l]) * (1 + sc2) + sh2
        x = x + gt2 * (jnp.square(jax.nn.relu(h @ w_up[l])) @ w_down[l])
        convs.append(nb)
        deltas.append(ns.astype(x.dtype))
        vrows.append(vr)
    y = rmsnorm(x, final_norm_g)
    return y, jnp.stack(convs), jnp.stack(deltas), jnp.stack(vrows)


def setup_inputs(seed: int = 0) -> dict:
    key = jax.random.key(seed)
    ks = jax.random.split(key, 24)
    f32 = jnp.float32
    nrm = lambda k, s, sc: jax.random.normal(k, s, f32) * sc
    dt = jnp.exp(jax.random.uniform(ks[15], (DEPTH, DN_HEADS), f32, np.log(1e-3), np.log(1e-1)))
    return {
        "x_prompt": nrm(ks[0], (BATCH, SEQ, D_MODEL), 1.0),
        "x_sample": nrm(ks[1], (DEC_BATCH, DEC_SEQ, D_MODEL), 1.0),
        "c_prompt": nrm(ks[2], (BATCH, D_MODEL), 1.0),
        "c_sample": nrm(ks[3], (DEC_BATCH, D_MODEL), 1.0),
        "state_conv": nrm(ks[4], (DEPTH, DEC_BATCH, CONV_W - 1, CONV_CH), 1.0),
        "state_delta": nrm(ks[5], (DEPTH, DEC_BATCH, DN_HEADS, DN_HEAD_DIM, DN_HEAD_DIM), 0.1),
        "ada_w": nrm(ks[6], (DEPTH, D_MODEL, 6 * D_MODEL), D_MODEL ** -0.5),
        "ada_b": nrm(ks[7], (DEPTH, 6 * D_MODEL), 0.02),
        "norm_mix_g": 1.0 + nrm(ks[8], (DEPTH, D_MODEL), 0.02),
        "norm_ffn_g": 1.0 + nrm(ks[9], (DEPTH, D_MODEL), 0.02),
        "w_in": nrm(ks[10], (DEPTH, D_MODEL, IN_WIDTH), D_MODEL ** -0.5),
        "sgu_norm_g": 1.0 + nrm(ks[11], (DEPTH, SGU_WIDTH), 0.02),
        "sgu_w": nrm(ks[12], (DEPTH, SGU_GROUPS, SGU_CHUNK, SGU_CHUNK), SGU_CHUNK ** -0.5),
        "sgu_b": nrm(ks[13], (DEPTH, SGU_GROUPS, SGU_CHUNK), 0.02),
        "conv_w": nrm(ks[14], (DEPTH, CONV_W, CONV_CH), CONV_W ** -0.5),
        "dt_bias": dt + jnp.log(-jnp.expm1(-dt)),
        "a_log": jnp.log(jax.random.uniform(ks[16], (DEPTH, DN_HEADS), f32, 1.0, 16.0)),
        "dn_norm_g": 1.0 + nrm(ks[17], (DEPTH, DN_HEAD_DIM), 0.02),
        "w_out": nrm(ks[18], (DEPTH, MIX_WIDTH, D_MODEL), MIX_WIDTH ** -0.5),
        "w_up": nrm(ks[19], (DEPTH, D_MODEL, D_FF), D_MODEL ** -0.5),
        "w_down": nrm(ks[20], (DEPTH, D_FF, D_MODEL), D_FF ** -0.5),
        "final_norm_g": 1.0 + nrm(ks[21], (D_MODEL,), 0.02),
    }


def reference(x_prompt, x_sample, c_prompt, c_sample, state_conv, state_delta, ada_w, ada_b, norm_mix_g,
              norm_ffn_g, w_in, sgu_norm_g, sgu_w, sgu_b, conv_w, dt_bias, a_log, dn_norm_g, w_out, w_up,
              w_down, final_norm_g):
    conv0 = jnp.zeros((DEPTH, x_prompt.shape[0], CONV_W - 1, CONV_CH), x_prompt.dtype)
    delta0 = jnp.zeros((DEPTH, x_prompt.shape[0], DN_HEADS, DN_HEAD_DIM, DN_HEAD_DIM), jnp.float32)
    y_prompt, prompt_conv, prompt_delta, _ = trunk(
        x_prompt, c_prompt, conv0, delta0, ada_w, ada_b, norm_mix_g, norm_ffn_g, w_in, sgu_norm_g,
        sgu_w, sgu_b, conv_w, dt_bias, a_log, dn_norm_g, w_out, w_up, w_down, final_norm_g)
    y_sample, sample_conv, sample_delta, sample_sgu_v = trunk(
        x_sample, c_sample, state_conv, state_delta, ada_w, ada_b, norm_mix_g, norm_ffn_g, w_in, sgu_norm_g,
        sgu_w, sgu_b, conv_w, dt_bias, a_log, dn_norm_g, w_out, w_up, w_down, final_norm_g)
    return (y_prompt, y_sample, prompt_conv, prompt_delta, sample_conv, sample_delta, sample_sgu_v)
```

```python
import functools

import jax
import jax.numpy as jnp
from jax import lax
from jax.experimental import pallas as pl
from jax.experimental.pallas import tpu as pltpu

F32 = jnp.float32
BF16 = jnp.bfloat16

D_MODEL = 1024
DEPTH = 2
CHUNK = 64
CHUNK_LOG2 = 6
SGU_WIDTH = 512
SGU_GROUPS = 4
GROUP_DIM = 128
DN_WIDTH = 512
DN_HEADS = 4
HEAD_DIM = 128
CONV_W = 4
CONV_CH = 3 * DN_WIDTH
D_FF = 4 * D_MODEL
EPS = 1e-6

O_UV = 0
O_QKV = 2 * SGU_WIDTH
O_Z = O_QKV + CONV_CH
O_AB = O_Z + DN_WIDTH
IN_EXT = O_AB + 2 * DN_HEADS * HEAD_DIM
CONV_PAD = 8

VMEM_LIMIT_BYTES = 56 * 1024 * 1024


def _sigmoid(x):
    return 1.0 / (1.0 + jnp.exp(-x))


def _silu(x):
    return x * _sigmoid(x)


def _softplus(x):
    return jnp.maximum(x, 0.0) + jnp.log(1.0 + jnp.exp(-jnp.abs(x)))


def _gelu_tanh(x):
    c = 0.7978845608028654
    return 0.5 * x * (1.0 + jnp.tanh(c * (x + 0.044715 * (x * x * x))))


def _dot(a, b):
    return jnp.dot(a, b, preferred_element_type=F32)


def _dot_nt(a, b):
    return lax.dot_general(a, b, (((1,), (1,)), ((), ())), preferred_element_type=F32)


def _split3(x):
    hi = x.astype(BF16)
    r = x - hi.astype(F32)
    mid = r.astype(BF16)
    lo = (r - mid.astype(F32)).astype(BF16)
    return hi, mid, lo


def _blockdiag_lanes64(y):
    lane = lax.broadcasted_iota(jnp.int32, y.shape, 1)
    top = jnp.where(lane < CHUNK, y, 0.0)
    bot = jnp.where(lane >= CHUNK, y, 0.0)
    return jnp.concatenate([top, bot], axis=0).astype(BF16)


def _blockdiag_lanes128(y):
    yb = y.astype(BF16)
    z = jnp.zeros((y.shape[0], HEAD_DIM), BF16)
    top = jnp.concatenate([yb[:, :HEAD_DIM], z], axis=1)
    bot = jnp.concatenate([z, yb[:, HEAD_DIM:]], axis=1)
    return jnp.concatenate([top, bot], axis=0)


MOD_BLOCK = 1024


def _mod_kernel(c_ref, w_ref, b_ref, o_ref):
    cs = _silu(c_ref[...]).astype(BF16)
    o_ref[0] = _dot(cs, w_ref[0].astype(BF16)) + b_ref[0]


def _modulation(c_all, ada_w, ada_b):
    n = c_all.shape[0]
    width = ada_w.shape[2]
    return pl.pallas_call(
        _mod_kernel,
        grid=(DEPTH, width // MOD_BLOCK),
        in_specs=[
            pl.BlockSpec((n, D_MODEL), lambda l, j: (0, 0)),
            pl.BlockSpec((1, D_MODEL, MOD_BLOCK), lambda l, j: (l, 0, j)),
            pl.BlockSpec((1, 1, MOD_BLOCK), lambda l, j: (l, 0, j)),
        ],
        out_specs=pl.BlockSpec((1, n, MOD_BLOCK), lambda l, j: (l, 0, j)),
        out_shape=jax.ShapeDtypeStruct((DEPTH, n, width), F32),
        compiler_params=pltpu.CompilerParams(
            dimension_semantics=("arbitrary", "arbitrary"), vmem_limit_bytes=VMEM_LIMIT_BYTES),
        name="adaln_modulation",
    )(c_all, ada_w, ada_b.reshape(DEPTH, 1, width))


def _mixer_kernel(x_ref, mod_ref, conv0_ref, delta0_ref, ng_ref, win_ref, sgng_ref, sgw_ref, sgb_ref,
                  cw_ref, dtb_ref, alog_ref, dng_ref, wout_ref,
                  xo_ref, convo_ref, deltao_ref, *rest,
                  bg, p, pv, emit_v):
    if emit_v:
        vo_ref, cbuf, act, abl, zs, mixb = rest
    else:
        cbuf, act, abl, zs, mixb = rest
    rows = bg * p
    nchunk = p // CHUNK
    step = pl.program_id(1)

    @pl.when(step == 0)
    def _():
        convo_ref[...] = conv0_ref[...]
        deltao_ref[...] = delta0_ref[...]

    x3 = x_ref[...]
    ms = jnp.mean(x3 * x3, axis=-1, keepdims=True)
    mod = mod_ref[...]
    h3 = (x3 * lax.rsqrt(ms + EPS)) * ng_ref[...] * (1.0 + mod[:, 1:2, :]) + mod[:, 0:1, :]
    hb = h3.reshape(rows, D_MODEL).astype(BF16)

    uv = _gelu_tanh(_dot(hb, win_ref[:, O_UV:O_QKV]))
    u = uv[:, :SGU_WIDTH]
    v = uv[:, SGU_WIDTH:]
    vms = jnp.mean(v * v, axis=-1, keepdims=True)
    vn = (v * lax.rsqrt(vms + EPS)) * sgng_ref[...]
    if emit_v:
        vo_ref[...] = vn.reshape(bg, p, SGU_WIDTH)
    ri = lax.broadcasted_iota(jnp.int32, (p, p), 0)
    ci = lax.broadcasted_iota(jnp.int32, (p, p), 1)
    sg_mask = (ci >> CHUNK_LOG2) <= (ri >> CHUNK_LOG2)
    vnb = vn.astype(BF16)
    for g in range(SGU_GROUPS):
        wg = jnp.where(sg_mask, sgw_ref[g], 0.0).astype(BF16)
        bias = sgb_ref[g]
        lo, hi = g * GROUP_DIM, (g + 1) * GROUP_DIM
        for b in range(bg):
            r0 = b * p
            s = _dot(wg, vnb[r0:r0 + p, lo:hi]) + bias
            mixb[r0:r0 + p, lo:hi] = (u[r0:r0 + p, lo:hi] * s).astype(BF16)

    cbuf[:, 0:CONV_PAD, :] = convo_ref[...]
    cbuf[:, CONV_PAD:CONV_PAD + p, :] = _dot(hb, win_ref[:, O_QKV:O_Z]).reshape(bg, p, CONV_CH)
    cw = cw_ref[...]
    y = cbuf[:, CONV_PAD:CONV_PAD + p, :] * cw[CONV_W - 1:CONV_W, :]
    for i in range(1, CONV_W):
        y = y + cbuf[:, CONV_PAD - i:CONV_PAD - i + p, :] * cw[CONV_W - 1 - i:CONV_W - i, :]
    convo_ref[...] = cbuf[:, pv:pv + CONV_PAD, :]
    a3 = _silu(y)
    act[...] = a3.reshape(rows, CONV_CH)
    for hh in range(2 * DN_HEADS):
        lo, hi = hh * HEAD_DIM, (hh + 1) * HEAD_DIM
        t = act[:, lo:hi]
        scale = HEAD_DIM ** -0.5 if hh < DN_HEADS else 1.0
        act[:, lo:hi] = t * (lax.rsqrt(jnp.sum(t * t, axis=-1, keepdims=True) + EPS) * scale)
    zs[...] = _silu(_dot(hb, win_ref[:, O_Z:O_AB]))
    abl[...] = _dot(hb, win_ref[:, O_AB:IN_EXT])

    ii = lax.broadcasted_iota(jnp.int32, (CHUNK, 2 * HEAD_DIM), 0)
    jj = lax.broadcasted_iota(jnp.int32, (CHUNK, 2 * HEAD_DIM), 1) & (CHUNK - 1)
    incl = ii >= jj
    strict = ii > jj
    upper = ii <= jj
    ii2 = lax.broadcasted_iota(jnp.int32, (CHUNK, HEAD_DIM), 0)
    lane2 = lax.broadcasted_iota(jnp.int32, (CHUNK, HEAD_DIM), 1)
    jj2 = lane2 & (CHUNK - 1)
    left2 = lane2 < CHUNK
    eye2 = jnp.where(ii2 == jj2, 1.0, 0.0).astype(F32)
    kk = lax.broadcasted_iota(jnp.int32, (CHUNK, 3 * CHUNK), 1) & (CHUNK - 1)
    rr = lax.broadcasted_iota(jnp.int32, (CHUNK, 3 * CHUNK), 0)
    ltri3 = jnp.where(kk <= rr, 1.0, 0.0).astype(BF16)
    ones3 = jnp.ones((CHUNK, 3 * CHUNK), BF16)
    level_masks = []
    for k in range(CHUNK_LOG2):
        level_masks.append(jnp.where(((ii2 ^ jj2) >> k) == 1, (ii2 >> k) & 1, 0) == 1)
    dtb = dtb_ref[...]
    neg_decay = -jnp.exp(alog_ref[...])
    dng = dng_ref[...]
    pos = lax.broadcasted_iota(jnp.int32, (CHUNK, DN_WIDTH), 0)

    def pack4(t):
        c0 = jnp.where(left2, t[:, 0:128], t[:, 128:256])
        c1 = jnp.where(left2, t[:, 256:384], t[:, 384:512])
        return jnp.concatenate([c0, c1], axis=1)

    pre = {}
    for b in range(bg):
        for c in range(nchunk):
            r0 = b * p + c * CHUNK
            beta = _sigmoid(abl[r0:r0 + CHUNK, 0:DN_WIDTH])
            g = neg_decay * _softplus(abl[r0:r0 + CHUNK, DN_WIDTH:2 * DN_WIDTH] + dtb)
            if pv < p:
                valid = (pos + c * CHUNK) < pv
                beta = jnp.where(valid, beta, 0.0)
                g = jnp.where(valid, g, 0.0)
            ghi, gmid, glo = _split3(g)
            gc = _dot(ltri3, jnp.concatenate([ghi, gmid, glo], axis=0))
            g4 = pack4(g)
            p4 = _split3(jnp.where(upper, g4, 0.0))
            grow4 = _dot(ones3, jnp.concatenate(p4, axis=0))
            diff4 = pack4(gc) - grow4
            dec_incl = jnp.exp(jnp.where(incl, diff4, -jnp.inf))
            dec_strict = jnp.where(strict, dec_incl, 0.0)
            egc = jnp.exp(gc)
            glast = gc[CHUNK - 1:CHUNK, :]
            ekg = jnp.exp(glast - gc)
            qn = act[r0:r0 + CHUNK, 0:DN_WIDTH]
            kn = act[r0:r0 + CHUNK, DN_WIDTH:2 * DN_WIDTH]
            vv = act[r0:r0 + CHUNK, 2 * DN_WIDTH:3 * DN_WIDTH]
            qb = qn.astype(BF16)
            kb = kn.astype(BF16)
            qk_cols, kk_cols = [], []
            for pr in range(DN_HEADS // 2):
                lo, hi = pr * 2 * HEAD_DIM, (pr + 1) * 2 * HEAD_DIM
                lhs = jnp.concatenate([qb[:, lo:hi], kb[:, lo:hi]], axis=0)
                r = _dot_nt(lhs, _blockdiag_lanes128(kn[:, lo:hi]))
                qk_cols.append(r[:CHUNK])
                kk_cols.append(r[CHUNK:])
            attn4 = jnp.concatenate(qk_cols, axis=1) * dec_incl
            a4 = jnp.concatenate(kk_cols, axis=1) * (pack4(beta) * dec_strict)
            pre[(b, c)] = dict(a4=a4, attn4=attn4.astype(BF16), beta=beta, vv=vv,
                               kgc=(kn * egc).astype(BF16), qg=(qn * egc).astype(BF16),
                               kg=kn * ekg, gl=jnp.exp(glast))

    probs = [(b, c, pr) for b in range(bg) for c in range(nchunk) for pr in range(DN_HEADS // 2)]
    a2 = {(b, c, pr): pre[(b, c)]["a4"][:, pr * HEAD_DIM:(pr + 1) * HEAD_DIM] for (b, c, pr) in probs}
    dinv = {key: eye2 - jnp.where(level_masks[0], a2[key], 0.0) for key in probs}
    for k in range(1, CHUNK_LOG2):
        mk = level_masks[k]
        p1 = {key: _dot(dinv[key].astype(BF16), _blockdiag_lanes64(jnp.where(mk, a2[key], 0.0)))
              for key in probs}
        p2 = {key: _dot(p1[key].astype(BF16), _blockdiag_lanes64(dinv[key])) for key in probs}
        dinv = {key: dinv[key] - p2[key] for key in probs}

    for c in range(nchunk):
        for b in range(bg):
            pc = pre[(b, c)]
            o_cols = []
            for pr in range(DN_HEADS // 2):
                lo, hi = pr * 2 * HEAD_DIM, (pr + 1) * 2 * HEAD_DIM
                s_pair = jnp.concatenate([deltao_ref[b, 2 * pr], deltao_ref[b, 2 * pr + 1]], axis=1)
                lhs = jnp.concatenate([pc["kgc"][:, lo:hi], pc["qg"][:, lo:hi]], axis=0)
                r = _dot(lhs, _blockdiag_lanes128(s_pair))
                xr = pc["beta"][:, lo:hi] * (pc["vv"][:, lo:hi] - r[:CHUNK])
                v_new = _dot(dinv[(b, c, pr)].astype(BF16), _blockdiag_lanes128(xr))
                kg_pair = jnp.concatenate([pc["kg"][:, lo:lo + HEAD_DIM], pc["kg"][:, lo + HEAD_DIM:hi]],
                                          axis=0)
                lhs2 = jnp.concatenate([pc["attn4"][:, pr * HEAD_DIM:(pr + 1) * HEAD_DIM],
                                        kg_pair.T.astype(BF16)], axis=0)
                yy = _dot(lhs2, _blockdiag_lanes128(v_new))
                o_cols.append(r[CHUNK:] + yy[:CHUNK])
                s_new = s_pair * pc["gl"][:, lo:hi] + yy[CHUNK:]
                deltao_ref[b, 2 * pr] = s_new[:, :HEAD_DIM]
                deltao_ref[b, 2 * pr + 1] = s_new[:, HEAD_DIM:]
            r0 = b * p + c * CHUNK
            for hh in range(DN_HEADS):
                o_h = o_cols[hh // 2][:, (hh % 2) * HEAD_DIM:(hh % 2 + 1) * HEAD_DIM]
                oms = jnp.mean(o_h * o_h, axis=-1, keepdims=True)
                gated = (o_h * lax.rsqrt(oms + EPS)) * dng * zs[r0:r0 + CHUNK, hh * HEAD_DIM:(hh + 1) * HEAD_DIM]
                mixb[r0:r0 + CHUNK, SGU_WIDTH + hh * HEAD_DIM:SGU_WIDTH + (hh + 1) * HEAD_DIM] = gated.astype(BF16)

    out = _dot(mixb[...], wout_ref[...])
    xo_ref[...] = x_ref[...] + mod[:, 2:3, :] * out.reshape(bg, p, D_MODEL)


def _mixer(x, mod, conv0, delta0, ng, win, sgng, sgw, sgb, cw, dtb, alog, dng, wout, *, bg, p, pv, emit_v):
    batch, seq, _ = x.shape
    grid = (batch // bg, seq // p)
    rows = bg * p
    full = lambda shape: pl.BlockSpec(shape, lambda i, j: (0,) * len(shape))
    in_specs = [
        pl.BlockSpec((bg, p, D_MODEL), lambda i, j: (i, j, 0)),
        pl.BlockSpec((bg, 6, D_MODEL), lambda i, j: (i, 0, 0)),
        pl.BlockSpec((bg, CONV_PAD, CONV_CH), lambda i, j: (i, 0, 0)),
        pl.BlockSpec((bg, DN_HEADS, HEAD_DIM, HEAD_DIM), lambda i, j: (i, 0, 0, 0)),
        full((1, D_MODEL)), full((D_MODEL, IN_EXT)), full((1, SGU_WIDTH)),
        full((SGU_GROUPS, p, p)), full((SGU_GROUPS, p, GROUP_DIM)),
        full((CONV_W, CONV_CH)), full((1, DN_WIDTH)), full((1, DN_WIDTH)), full((1, HEAD_DIM)),
        full((D_MODEL, D_MODEL)),
    ]
    out_specs = [
        pl.BlockSpec((bg, p, D_MODEL), lambda i, j: (i, j, 0)),
        pl.BlockSpec((bg, CONV_PAD, CONV_CH), lambda i, j: (i, 0, 0)),
        pl.BlockSpec((bg, DN_HEADS, HEAD_DIM, HEAD_DIM), lambda i, j: (i, 0, 0, 0)),
    ]
    out_shape = [
        jax.ShapeDtypeStruct((batch, seq, D_MODEL), F32),
        jax.ShapeDtypeStruct((batch, CONV_PAD, CONV_CH), F32),
        jax.ShapeDtypeStruct((batch, DN_HEADS, HEAD_DIM, HEAD_DIM), F32),
    ]
    if emit_v:
        out_specs.append(pl.BlockSpec((bg, p, SGU_WIDTH), lambda i, j: (i, j, 0)))
        out_shape.append(jax.ShapeDtypeStruct((batch, seq, SGU_WIDTH), F32))
    scratch = [
        pltpu.VMEM((bg, CONV_PAD + p, CONV_CH), F32),
        pltpu.VMEM((rows, CONV_CH), F32),
        pltpu.VMEM((rows, 2 * DN_WIDTH), F32),
        pltpu.VMEM((rows, DN_WIDTH), F32),
        pltpu.VMEM((rows, D_MODEL), BF16),
    ]
    return pl.pallas_call(
        functools.partial(_mixer_kernel, bg=bg, p=p, pv=pv, emit_v=emit_v),
        grid=grid, in_specs=in_specs, out_specs=out_specs, out_shape=out_shape,
        scratch_shapes=scratch,
        compiler_params=pltpu.CompilerParams(
            dimension_semantics=("arbitrary", "arbitrary"), vmem_limit_bytes=VMEM_LIMIT_BYTES),
        name="mixer",
    )(x, mod, conv0, delta0, ng, win, sgng, sgw, sgb, cw, dtb, alog, dng, wout)


def _ffn_kernel(x_ref, mod_ref, ng_ref, wup_ref, wdn_ref, fg_ref, o_ref, *, bb, tm, final_norm):
    x3 = x_ref[...]
    mod = mod_ref[...]
    ms = jnp.mean(x3 * x3, axis=-1, keepdims=True)
    h3 = (x3 * lax.rsqrt(ms + EPS)) * ng_ref[...] * (1.0 + mod[:, 4:5, :]) + mod[:, 3:4, :]
    hb = h3.reshape(bb * tm, D_MODEL).astype(BF16)
    a = jnp.maximum(_dot(hb, wup_ref[...]), 0.0)
    a = (a * a).astype(BF16)
    y = x3 + mod[:, 5:6, :] * _dot(a, wdn_ref[...]).reshape(bb, tm, D_MODEL)
    if final_norm:
        yms = jnp.mean(y * y, axis=-1, keepdims=True)
        y = (y * lax.rsqrt(yms + EPS)) * fg_ref[...]
    o_ref[...] = y


def _ffn(x, mod, ng, wup, wdn, fg, *, bb, tm, final_norm):
    batch, seq, _ = x.shape
    full = lambda shape: pl.BlockSpec(shape, lambda i, j: (0,) * len(shape))
    return pl.pallas_call(
        functools.partial(_ffn_kernel, bb=bb, tm=tm, final_norm=final_norm),
        grid=(batch // bb, seq // tm),
        in_specs=[
            pl.BlockSpec((bb, tm, D_MODEL), lambda i, j: (i, j, 0)),
            pl.BlockSpec((bb, 6, D_MODEL), lambda i, j: (i, 0, 0)),
            full((1, D_MODEL)), full((D_MODEL, D_FF)), full((D_FF, D_MODEL)), full((1, D_MODEL)),
        ],
        out_specs=pl.BlockSpec((bb, tm, D_MODEL), lambda i, j: (i, j, 0)),
        out_shape=jax.ShapeDtypeStruct((batch, seq, D_MODEL), F32),
        compiler_params=pltpu.CompilerParams(
            dimension_semantics=("arbitrary", "arbitrary"), vmem_limit_bytes=VMEM_LIMIT_BYTES),
        name="ffn",
    )(x, mod, ng, wup, wdn, fg)


def _extend_w_in(w_in_l):
    ab = w_in_l[:, O_AB:O_AB + 2 * DN_HEADS]
    return jnp.concatenate([w_in_l[:, :O_AB], jnp.repeat(ab, HEAD_DIM, axis=1)], axis=1).astype(BF16)


def _trunk(x, mod_all, conv_state, delta_state, params, *, bg, p, pv, ffn_bb, ffn_tm, emit_v):
    batch = x.shape[0]
    convs, deltas, vrows = [], [], []
    pad = p - pv
    for l in range(DEPTH):
        mod = mod_all[l].reshape(batch, 6, D_MODEL)
        conv0 = jnp.pad(conv_state[l], ((0, 0), (CONV_PAD - (CONV_W - 1), 0), (0, 0)))
        sgw = jnp.pad(params["sgu_w"][l][:, :pv, :pv], ((0, 0), (0, pad), (0, pad)))
        sgb = jnp.pad(params["sgu_b"][l][:, :pv], ((0, 0), (0, pad)))
        sgb = jnp.broadcast_to(sgb[:, :, None], (SGU_GROUPS, p, GROUP_DIM))
        outs = _mixer(
            x, mod, conv0, delta_state[l],
            params["norm_mix_g"][l][None, :], params["w_in_ext"][l], params["sgu_norm_g"][l][None, :],
            sgw, sgb, params["conv_w"][l],
            jnp.repeat(params["dt_bias"][l], HEAD_DIM)[None, :],
            jnp.repeat(params["a_log"][l], HEAD_DIM)[None, :],
            params["dn_norm_g"][l][None, :], params["w_out_bf"][l],
            bg=bg, p=p, pv=pv, emit_v=emit_v)
        x1 = outs[0]
        convs.append(outs[1][:, CONV_PAD - (CONV_W - 1):, :])
        deltas.append(outs[2])
        if emit_v:
            vrows.append(outs[3][:, :pv, :])
        if pad:
            x1 = x1[:, :pv, :]
        x2 = _ffn(x1, mod, params["norm_ffn_g"][l][None, :], params["w_up_bf"][l], params["w_down_bf"][l],
                  params["final_norm_g"][None, :], bb=ffn_bb, tm=ffn_tm, final_norm=(l == DEPTH - 1))
        if pad and l + 1 < DEPTH:
            x = jnp.pad(x2, ((0, 0), (0, pad), (0, 0)))
        else:
            x = x2
    return x, jnp.stack(convs), jnp.stack(deltas), (jnp.stack(vrows) if emit_v else None)


def kernel(x_prompt, x_sample, c_prompt, c_sample, state_conv, state_delta, ada_w, ada_b, norm_mix_g,
           norm_ffn_g, w_in, sgu_norm_g, sgu_w, sgu_b, conv_w, dt_bias, a_log, dn_norm_g, w_out, w_up,
           w_down, final_norm_g):
    nb, seq, _ = x_prompt.shape
    ns, dec_seq, _ = x_sample.shape
    params = dict(
        norm_mix_g=norm_mix_g, norm_ffn_g=norm_ffn_g, sgu_norm_g=sgu_norm_g, sgu_w=sgu_w, sgu_b=sgu_b,
        conv_w=conv_w, dt_bias=dt_bias, a_log=a_log, dn_norm_g=dn_norm_g, final_norm_g=final_norm_g,
        w_in_ext=[_extend_w_in(w_in[l]) for l in range(DEPTH)],
        w_out_bf=w_out.astype(BF16), w_up_bf=w_up.astype(BF16), w_down_bf=w_down.astype(BF16))

    mod_all = _modulation(jnp.concatenate([c_prompt, c_sample], axis=0), ada_w, ada_b)

    conv0 = jnp.zeros((DEPTH, nb, CONV_W - 1, CONV_CH), F32)
    delta0 = jnp.zeros((DEPTH, nb, DN_HEADS, HEAD_DIM, HEAD_DIM), F32)
    y_prompt, prompt_conv, prompt_delta, _ = _trunk(
        x_prompt, mod_all[:, :nb], conv0, delta0, params,
        bg=4, p=2 * CHUNK, pv=2 * CHUNK, ffn_bb=1, ffn_tm=512, emit_v=False)

    xs = jnp.pad(x_sample, ((0, 0), (0, CHUNK - dec_seq), (0, 0)))
    y_sample, sample_conv, sample_delta, sample_v = _trunk(
        xs, mod_all[:, nb:], state_conv, state_delta, params,
        bg=ns, p=CHUNK, pv=dec_seq, ffn_bb=ns, ffn_tm=dec_seq, emit_v=True)

    return (y_prompt, y_sample, prompt_conv, prompt_delta, sample_conv, sample_delta, sample_v)
```

```python
import functools

import jax
import jax.numpy as jnp
from jax import lax
from jax.experimental import pallas as pl
from jax.experimental.pallas import tpu as pltpu

F32 = jnp.float32
BF16 = jnp.bfloat16

D_MODEL = 1024
DEPTH = 2
CHUNK = 64
CHUNK_LOG2 = 6
SGU_WIDTH = 512
SGU_GROUPS = 4
GROUP_DIM = 128
DN_WIDTH = 512
DN_HEADS = 4
HEAD_DIM = 128
CONV_W = 4
CONV_CH = 3 * DN_WIDTH
D_FF = 4 * D_MODEL
EPS = 1e-6

O_UV = 0
O_QKV = 2 * SGU_WIDTH
O_Z = O_QKV + CONV_CH
O_AB = O_Z + DN_WIDTH
IN_EXT = O_AB + 2 * DN_HEADS * HEAD_DIM
CONV_PAD = 8

VMEM_LIMIT_BYTES = 56 * 1024 * 1024


def _sigmoid(x):
    return 0.5 * (1.0 + jnp.tanh(0.5 * x))


def _silu(x):
    return x * _sigmoid(x)


def _softplus(x):
    return jnp.maximum(x, 0.0) + jnp.log(1.0 + jnp.exp(-jnp.abs(x)))


def _gelu_tanh(x):
    c = 0.7978845608028654
    return 0.5 * x * (1.0 + jnp.tanh(c * (x + 0.044715 * (x * x * x))))


def _dot(a, b):
    return jnp.dot(a, b, preferred_element_type=F32)


def _dot_nt(a, b):
    return lax.dot_general(a, b, (((1,), (1,)), ((), ())), preferred_element_type=F32)


def _split3(x):
    hi = x.astype(BF16)
    r = x - hi.astype(F32)
    mid = r.astype(BF16)
    lo = (r - mid.astype(F32)).astype(BF16)
    return hi, mid, lo


def _blockdiag_lanes64(y):
    lane = lax.broadcasted_iota(jnp.int32, y.shape, 1)
    top = jnp.where(lane < CHUNK, y, 0.0)
    bot = jnp.where(lane >= CHUNK, y, 0.0)
    return jnp.concatenate([top, bot], axis=0).astype(BF16)


def _blockdiag_lanes128(y):
    yb = y.astype(BF16)
    z = jnp.zeros((y.shape[0], HEAD_DIM), BF16)
    top = jnp.concatenate([yb[:, :HEAD_DIM], z], axis=1)
    bot = jnp.concatenate([z, yb[:, HEAD_DIM:]], axis=1)
    return jnp.concatenate([top, bot], axis=0)


def _layer_spec(shape, layer):
    return pl.BlockSpec((None,) + tuple(shape), lambda i, j: (layer,) + (0,) * len(shape))


MOD_BLOCK = 1024


def _mod_kernel(c_ref, w_ref, b_ref, o_ref):
    cs = _silu(c_ref[...]).astype(BF16)
    o_ref[...] = _dot(cs, w_ref[...].astype(BF16)) + b_ref[...]


def _modulation(c_all, ada_w, ada_b):
    n = c_all.shape[0]
    width = ada_w.shape[2]
    return pl.pallas_call(
        _mod_kernel,
        grid=(DEPTH, width // MOD_BLOCK),
        in_specs=[
            pl.BlockSpec((n, D_MODEL), lambda l, j: (0, 0)),
            pl.BlockSpec((None, D_MODEL, MOD_BLOCK), lambda l, j: (l, 0, j)),
            pl.BlockSpec((None, 1, MOD_BLOCK), lambda l, j: (l, 0, j)),
        ],
        out_specs=pl.BlockSpec((None, n, MOD_BLOCK), lambda l, j: (l, 0, j)),
        out_shape=jax.ShapeDtypeStruct((DEPTH, n, width), F32),
        compiler_params=pltpu.CompilerParams(
            dimension_semantics=("arbitrary", "arbitrary"), vmem_limit_bytes=VMEM_LIMIT_BYTES),
        name="adaln_modulation",
    )(c_all, ada_w, ada_b.reshape(DEPTH, 1, width))


def _mixer_kernel(*refs, bg, p, pv, emit_v, zero_init):
    refs = list(refs)
    x_ref, mod_ref = refs[:2]
    refs = refs[2:]
    if not zero_init:
        conv0_ref, delta0_ref = refs[:2]
        refs = refs[2:]
    (ng_ref, win_ref, sgng_ref, sgw_ref, sgb_ref, cw_ref, dtb_ref, alog_ref, dng_ref, wout_ref,
     xo_ref, convo_ref, deltao_ref) = refs[:13]
    refs = refs[13:]
    if emit_v:
        vo_ref = refs[0]
        refs = refs[1:]
    ctail, cbuf, act, abl, zs, mixb = refs
    rows = bg * p
    nchunk = p // CHUNK
    step = pl.program_id(1)

    @pl.when(step == 0)
    def _():
        if zero_init:
            ctail[...] = jnp.zeros(ctail.shape, F32)
            deltao_ref[...] = jnp.zeros(deltao_ref.shape, F32)
        else:
            ctail[...] = conv0_ref[...]
            deltao_ref[...] = delta0_ref[...]

    x3 = x_ref[...]
    ms = jnp.mean(x3 * x3, axis=-1, keepdims=True)
    mod = mod_ref[...]
    h3 = (x3 * lax.rsqrt(ms + EPS)) * ng_ref[...] * (1.0 + mod[:, 1:2, :]) + mod[:, 0:1, :]
    if pv < p:
        h3 = jnp.concatenate([h3, jnp.zeros((bg, p - pv, D_MODEL), F32)], axis=1)
    hb = h3.reshape(rows, D_MODEL).astype(BF16)

    uv = _gelu_tanh(_dot(hb, win_ref[:, O_UV:O_QKV]))
    u = uv[:, :SGU_WIDTH]
    v = uv[:, SGU_WIDTH:]
    vms = jnp.mean(v * v, axis=-1, keepdims=True)
    vn = (v * lax.rsqrt(vms + EPS)) * sgng_ref[...]
    if emit_v:
        vo_ref[...] = vn.reshape(bg, p, SGU_WIDTH)[:, :pv, :]
    ri = lax.broadcasted_iota(jnp.int32, (p, p), 0)
    ci = lax.broadcasted_iota(jnp.int32, (p, p), 1)
    sg_mask = (ci >> CHUNK_LOG2) <= (ri >> CHUNK_LOG2)
    vnb = vn.astype(BF16)
    for g in range(SGU_GROUPS):
        wg = jnp.where(sg_mask, sgw_ref[g], 0.0).astype(BF16)
        bias = sgb_ref[g]
        lo, hi = g * GROUP_DIM, (g + 1) * GROUP_DIM
        for b in range(bg):
            r0 = b * p
            s = _dot(wg, vnb[r0:r0 + p, lo:hi]) + bias
            mixb[r0:r0 + p, lo:hi] = (u[r0:r0 + p, lo:hi] * s).astype(BF16)

    cbuf[:, 0:CONV_PAD, :] = ctail[...]
    cbuf[:, CONV_PAD:CONV_PAD + p, :] = _dot(hb, win_ref[:, O_QKV:O_Z]).reshape(bg, p, CONV_CH)
    cw = cw_ref[...]
    y = cbuf[:, CONV_PAD:CONV_PAD + p, :] * cw[CONV_W - 1:CONV_W, :]
    for i in range(1, CONV_W):
        y = y + cbuf[:, CONV_PAD - i:CONV_PAD - i + p, :] * cw[CONV_W - 1 - i:CONV_W - i, :]
    ctail[...] = cbuf[:, pv:pv + CONV_PAD, :]
    convo_ref[...] = cbuf[:, pv + CONV_PAD - (CONV_W - 1):pv + CONV_PAD, :]
    a3 = _silu(y)
    act[...] = a3.reshape(rows, CONV_CH)
    for hh in range(2 * DN_HEADS):
        lo, hi = hh * HEAD_DIM, (hh + 1) * HEAD_DIM
        t = act[:, lo:hi]
        scale = HEAD_DIM ** -0.5 if hh < DN_HEADS else 1.0
        act[:, lo:hi] = t * (lax.rsqrt(jnp.sum(t * t, axis=-1, keepdims=True) + EPS) * scale)
    zs[...] = _silu(_dot(hb, win_ref[:, O_Z:O_AB]))
    abl[...] = _dot(hb, win_ref[:, O_AB:IN_EXT])

    ii = lax.broadcasted_iota(jnp.int32, (CHUNK, 2 * HEAD_DIM), 0)
    jj = lax.broadcasted_iota(jnp.int32, (CHUNK, 2 * HEAD_DIM), 1) & (CHUNK - 1)
    incl = ii >= jj
    strict = ii > jj
    upper = ii <= jj
    ii2 = lax.broadcasted_iota(jnp.int32, (CHUNK, HEAD_DIM), 0)
    lane2 = lax.broadcasted_iota(jnp.int32, (CHUNK, HEAD_DIM), 1)
    jj2 = lane2 & (CHUNK - 1)
    left2 = lane2 < CHUNK
    eye2 = jnp.where(ii2 == jj2, 1.0, 0.0).astype(F32)
    kk = lax.broadcasted_iota(jnp.int32, (CHUNK, 3 * CHUNK), 1) & (CHUNK - 1)
    rr = lax.broadcasted_iota(jnp.int32, (CHUNK, 3 * CHUNK), 0)
    ltri3 = jnp.where(kk <= rr, 1.0, 0.0).astype(BF16)
    ones3 = jnp.ones((CHUNK, 3 * CHUNK), BF16)
    level_masks = []
    for k in range(CHUNK_LOG2):
        level_masks.append(jnp.where(((ii2 ^ jj2) >> k) == 1, (ii2 >> k) & 1, 0) == 1)
    dtb = dtb_ref[...]
    neg_decay = -jnp.exp(alog_ref[...])
    dng = dng_ref[...]
    pos = lax.broadcasted_iota(jnp.int32, (CHUNK, DN_WIDTH), 0)

    def pack4(t):
        c0 = jnp.where(left2, t[:, 0:128], t[:, 128:256])
        c1 = jnp.where(left2, t[:, 256:384], t[:, 384:512])
        return jnp.concatenate([c0, c1], axis=1)

    pre = {}
    for b in range(bg):
        for c in range(nchunk):
            r0 = b * p + c * CHUNK
            beta = _sigmoid(abl[r0:r0 + CHUNK, 0:DN_WIDTH])
            g = neg_decay * _softplus(abl[r0:r0 + CHUNK, DN_WIDTH:2 * DN_WIDTH] + dtb)
            if pv < p:
                valid = (pos + c * CHUNK) < pv
                beta = jnp.where(valid, beta, 0.0)
                g = jnp.where(valid, g, 0.0)
            ghi, gmid, glo = _split3(g)
            gc = _dot(ltri3, jnp.concatenate([ghi, gmid, glo], axis=0))
            g4 = pack4(g)
            p4 = _split3(jnp.where(upper, g4, 0.0))
            grow4 = _dot(ones3, jnp.concatenate(p4, axis=0))
            diff4 = pack4(gc) - grow4
            dec_incl = jnp.exp(jnp.where(incl, diff4, -jnp.inf))
            dec_strict = jnp.where(strict, dec_incl, 0.0)
            egc = jnp.exp(gc)
            glast = gc[CHUNK - 1:CHUNK, :]
            ekg = jnp.exp(glast - gc)
            qn = act[r0:r0 + CHUNK, 0:DN_WIDTH]
            kn = act[r0:r0 + CHUNK, DN_WIDTH:2 * DN_WIDTH]
            vv = act[r0:r0 + CHUNK, 2 * DN_WIDTH:3 * DN_WIDTH]
            qb = qn.astype(BF16)
            kb = kn.astype(BF16)
            qk_cols, kk_cols = [], []
            for pr in range(DN_HEADS // 2):
                lo, hi = pr * 2 * HEAD_DIM, (pr + 1) * 2 * HEAD_DIM
                lhs = jnp.concatenate([qb[:, lo:hi], kb[:, lo:hi]], axis=0)
                r = _dot_nt(lhs, _blockdiag_lanes128(kn[:, lo:hi]))
                qk_cols.append(r[:CHUNK])
                kk_cols.append(r[CHUNK:])
            attn4 = jnp.concatenate(qk_cols, axis=1) * dec_incl
            a4 = jnp.concatenate(kk_cols, axis=1) * (pack4(beta) * dec_strict)
            pre[(b, c)] = dict(a4=a4, attn4=attn4.astype(BF16), beta=beta, vv=vv,
                               kgc=(kn * egc).astype(BF16), qg=(qn * egc).astype(BF16),
                               kg=kn * ekg, gl=jnp.exp(glast))

    probs = [(b, c, pr) for b in range(bg) for c in range(nchunk) for pr in range(DN_HEADS // 2)]
    a2 = {(b, c, pr): pre[(b, c)]["a4"][:, pr * HEAD_DIM:(pr + 1) * HEAD_DIM] for (b, c, pr) in probs}
    dinv = {key: eye2 - jnp.where(level_masks[0], a2[key], 0.0) for key in probs}
    for k in range(1, CHUNK_LOG2):
        mk = level_masks[k]
        p1 = {key: _dot(dinv[key].astype(BF16), _blockdiag_lanes64(jnp.where(mk, a2[key], 0.0)))
              for key in probs}
        p2 = {key: _dot(p1[key].astype(BF16), _blockdiag_lanes64(dinv[key])) for key in probs}
        dinv = {key: dinv[key] - p2[key] for key in probs}

    for c in range(nchunk):
        for b in range(bg):
            pc = pre[(b, c)]
            o_cols = []
            for pr in range(DN_HEADS // 2):
                lo, hi = pr * 2 * HEAD_DIM, (pr + 1) * 2 * HEAD_DIM
                s_pair = jnp.concatenate([deltao_ref[b, 2 * pr], deltao_ref[b, 2 * pr + 1]], axis=1)
                lhs = jnp.concatenate([pc["kgc"][:, lo:hi], pc["qg"][:, lo:hi]], axis=0)
                r = _dot(lhs, _blockdiag_lanes128(s_pair))
                xr = pc["beta"][:, lo:hi] * (pc["vv"][:, lo:hi] - r[:CHUNK])
                v_new = _dot(dinv[(b, c, pr)].astype(BF16), _blockdiag_lanes128(xr))
                kg_pair = jnp.concatenate([pc["kg"][:, lo:lo + HEAD_DIM], pc["kg"][:, lo + HEAD_DIM:hi]],
                                          axis=0)
                lhs2 = jnp.concatenate([pc["attn4"][:, pr * HEAD_DIM:(pr + 1) * HEAD_DIM],
                                        kg_pair.T.astype(BF16)], axis=0)
                yy = _dot(lhs2, _blockdiag_lanes128(v_new))
                o_cols.append(r[CHUNK:] + yy[:CHUNK])
                s_new = s_pair * pc["gl"][:, lo:hi] + yy[CHUNK:]
                deltao_ref[b, 2 * pr] = s_new[:, :HEAD_DIM]
                deltao_ref[b, 2 * pr + 1] = s_new[:, HEAD_DIM:]
            r0 = b * p + c * CHUNK
            for hh in range(DN_HEADS):
                o_h = o_cols[hh // 2][:, (hh % 2) * HEAD_DIM:(hh % 2 + 1) * HEAD_DIM]
                oms = jnp.mean(o_h * o_h, axis=-1, keepdims=True)
                gated = (o_h * lax.rsqrt(oms + EPS)) * dng * zs[r0:r0 + CHUNK, hh * HEAD_DIM:(hh + 1) * HEAD_DIM]
                mixb[r0:r0 + CHUNK, SGU_WIDTH + hh * HEAD_DIM:SGU_WIDTH + (hh + 1) * HEAD_DIM] = gated.astype(BF16)

    out = _dot(mixb[...], wout_ref[...]).reshape(bg, p, D_MODEL)
    xo_ref[...] = x_ref[...] + mod[:, 2:3, :] * out[:, :pv, :]


def _mixer(x, mod4, conv_in, delta_in, prm, sgw, sgb, *, layer, mod_off, bg, p, pv, emit_v):
    batch, seq, _ = x.shape
    zero_init = conv_in is None
    grid = (batch // bg, seq // pv)
    rows = bg * p
    mod_blk = mod_off // bg
    in_specs = [
        pl.BlockSpec((bg, pv, D_MODEL), lambda i, j: (i, j, 0)),
        pl.BlockSpec((None, bg, 6, D_MODEL), lambda i, j: (layer, i + mod_blk, 0, 0)),
    ]
    args = [x, mod4]
    if not zero_init:
        in_specs += [
            pl.BlockSpec((None, bg, CONV_PAD, CONV_CH), lambda i, j: (layer, i, 0, 0)),
            pl.BlockSpec((None, bg, DN_HEADS, HEAD_DIM, HEAD_DIM), lambda i, j: (layer, i, 0, 0, 0)),
        ]
        args += [conv_in, delta_in]
    in_specs += [
        _layer_spec((1, D_MODEL), layer), _layer_spec((D_MODEL, IN_EXT), layer),
        _layer_spec((1, SGU_WIDTH), layer), _layer_spec((SGU_GROUPS, p, p), layer),
        _layer_spec((SGU_GROUPS, p, GROUP_DIM), layer), _layer_spec((CONV_W, CONV_CH), layer),
        _layer_spec((1, DN_WIDTH), layer), _layer_spec((1, DN_WIDTH), layer),
        _layer_spec((1, HEAD_DIM), layer), _layer_spec((D_MODEL, D_MODEL), layer),
    ]
    args += [prm["ng_mix"], prm["win"], prm["sgng"], sgw, sgb, prm["cw"], prm["dtb"], prm["alog"],
             prm["dng"], prm["wout"]]
    out_specs = [
        pl.BlockSpec((bg, pv, D_MODEL), lambda i, j: (i, j, 0)),
        pl.BlockSpec((bg, CONV_W - 1, CONV_CH), lambda i, j: (i, 0, 0)),
        pl.BlockSpec((bg, DN_HEADS, HEAD_DIM, HEAD_DIM), lambda i, j: (i, 0, 0, 0)),
    ]
    out_shape = [
        jax.ShapeDtypeStruct((batch, seq, D_MODEL), F32),
        jax.ShapeDtypeStruct((batch, CONV_W - 1, CONV_CH), F32),
        jax.ShapeDtypeStruct((batch, DN_HEADS, HEAD_DIM, HEAD_DIM), F32),
    ]
    if emit_v:
        out_specs.append(pl.BlockSpec((bg, pv, SGU_WIDTH), lambda i, j: (i, j, 0)))
        out_shape.append(jax.ShapeDtypeStruct((batch, seq, SGU_WIDTH), F32))
    scratch = [
        pltpu.VMEM((bg, CONV_PAD, CONV_CH), F32),
        pltpu.VMEM((bg, CONV_PAD + p, CONV_CH), F32),
        pltpu.VMEM((rows, CONV_CH), F32),
        pltpu.VMEM((rows, 2 * DN_WIDTH), F32),
        pltpu.VMEM((rows, DN_WIDTH), F32),
        pltpu.VMEM((rows, D_MODEL), BF16),
    ]
    return pl.pallas_call(
        functools.partial(_mixer_kernel, bg=bg, p=p, pv=pv, emit_v=emit_v, zero_init=zero_init),
        grid=grid, in_specs=in_specs, out_specs=out_specs, out_shape=out_shape,
        scratch_shapes=scratch,
        compiler_params=pltpu.CompilerParams(
            dimension_semantics=("arbitrary", "arbitrary"), vmem_limit_bytes=VMEM_LIMIT_BYTES),
        name="mixer",
    )(*args)


def _ffn_kernel(x_ref, mod_ref, ng_ref, wup_ref, wdn_ref, fg_ref, o_ref, *, bb, tm, final_norm):
    x3 = x_ref[...]
    mod = mod_ref[...]
    ms = jnp.mean(x3 * x3, axis=-1, keepdims=True)
    h3 = (x3 * lax.rsqrt(ms + EPS)) * ng_ref[...] * (1.0 + mod[:, 4:5, :]) + mod[:, 3:4, :]
    hb = h3.reshape(bb * tm, D_MODEL).astype(BF16)
    a = jnp.maximum(_dot(hb, wup_ref[...]), 0.0)
    a = (a * a).astype(BF16)
    y = x3 + mod[:, 5:6, :] * _dot(a, wdn_ref[...]).reshape(bb, tm, D_MODEL)
    if final_norm:
        yms = jnp.mean(y * y, axis=-1, keepdims=True)
        y = (y * lax.rsqrt(yms + EPS)) * fg_ref[...]
    o_ref[...] = y


def _ffn(x, mod4, prm, *, layer, mod_off, bb, tm, final_norm):
    batch, seq, _ = x.shape
    mod_blk = mod_off // bb
    return pl.pallas_call(
        functools.partial(_ffn_kernel, bb=bb, tm=tm, final_norm=final_norm),
        grid=(batch // bb, seq // tm),
        in_specs=[
            pl.BlockSpec((bb, tm, D_MODEL), lambda i, j: (i, j, 0)),
            pl.BlockSpec((None, bb, 6, D_MODEL), lambda i, j: (layer, i + mod_blk, 0, 0)),
            _layer_spec((1, D_MODEL), layer), _layer_spec((D_MODEL, D_FF), layer),
            _layer_spec((D_FF, D_MODEL), layer),
            pl.BlockSpec((1, D_MODEL), lambda i, j: (0, 0)),
        ],
        out_specs=pl.BlockSpec((bb, tm, D_MODEL), lambda i, j: (i, j, 0)),
        out_shape=jax.ShapeDtypeStruct((batch, seq, D_MODEL), F32),
        compiler_params=pltpu.CompilerParams(
            dimension_semantics=("arbitrary", "arbitrary"), vmem_limit_bytes=VMEM_LIMIT_BYTES),
        name="ffn",
    )(x, mod4, prm["ng_ffn"], prm["wup"], prm["wdn"], prm["fg"])


def _trunk(x, mod4, conv_in, delta_in, prm, sgw, sgb, *, mod_off, bg, p, pv, ffn_bb, ffn_tm, emit_v):
    convs, deltas, vrows = [], [], []
    for l in range(DEPTH):
        outs = _mixer(x, mod4, conv_in, delta_in, prm, sgw, sgb, layer=l, mod_off=mod_off,
                      bg=bg, p=p, pv=pv, emit_v=emit_v)
        convs.append(outs[1])
        deltas.append(outs[2])
        if emit_v:
            vrows.append(outs[3])
        x = _ffn(outs[0], mod4, prm, layer=l, mod_off=mod_off, bb=ffn_bb, tm=ffn_tm,
                 final_norm=(l == DEPTH - 1))
    return x, jnp.stack(convs), jnp.stack(deltas), (jnp.stack(vrows) if emit_v else None)


def kernel(x_prompt, x_sample, c_prompt, c_sample, state_conv, state_delta, ada_w, ada_b, norm_mix_g,
           norm_ffn_g, w_in, sgu_norm_g, sgu_w, sgu_b, conv_w, dt_bias, a_log, dn_norm_g, w_out, w_up,
           w_down, final_norm_g):
    nb, seq, _ = x_prompt.shape
    ns, dec_seq, _ = x_sample.shape
    sgu_chunk = sgu_w.shape[-1]

    ab_cols = jnp.repeat(w_in[:, :, O_AB:O_AB + 2 * DN_HEADS], HEAD_DIM, axis=2)
    prm = dict(
        ng_mix=norm_mix_g[:, None, :], ng_ffn=norm_ffn_g[:, None, :], sgng=sgu_norm_g[:, None, :],
        win=jnp.concatenate([w_in[:, :, :O_AB], ab_cols], axis=2).astype(BF16),
        cw=conv_w, dtb=jnp.repeat(dt_bias, HEAD_DIM, axis=1)[:, None, :],
        alog=jnp.repeat(a_log, HEAD_DIM, axis=1)[:, None, :], dng=dn_norm_g[:, None, :],
        wout=w_out.astype(BF16), wup=w_up.astype(BF16), wdn=w_down.astype(BF16), fg=final_norm_g[None, :])
    sgb_p = jnp.broadcast_to(sgu_b[:, :, :, None], (DEPTH, SGU_GROUPS, sgu_chunk, GROUP_DIM))
    pad = CHUNK - dec_seq
    sgw_s = jnp.pad(sgu_w[:, :, :dec_seq, :dec_seq], ((0, 0), (0, 0), (0, pad), (0, pad)))
    sgb_s = jnp.pad(sgb_p[:, :, :dec_seq, :], ((0, 0), (0, 0), (0, pad), (0, 0)))
    conv_in = jnp.pad(state_conv, ((0, 0), (0, 0), (CONV_PAD - (CONV_W - 1), 0), (0, 0)))

    mod_all = _modulation(jnp.concatenate([c_prompt, c_sample], axis=0), ada_w, ada_b)
    mod4 = mod_all.reshape(DEPTH, nb + ns, 6, D_MODEL)

    y_prompt, prompt_conv, prompt_delta, _ = _trunk(
        x_prompt, mod4, None, None, prm, sgu_w, sgb_p,
        mod_off=0, bg=4, p=sgu_chunk, pv=sgu_chunk, ffn_bb=1, ffn_tm=512, emit_v=False)

    y_sample, sample_conv, sample_delta, sample_v = _trunk(
        x_sample, mod4, conv_in, state_delta, prm, sgw_s, sgb_s,
        mod_off=nb, bg=ns, p=CHUNK, pv=dec_seq, ffn_bb=ns, ffn_tm=dec_seq, emit_v=True)

    return (y_prompt, y_sample, prompt_conv, prompt_delta, sample_conv, sample_delta, sample_v)
```

```python
import functools

import jax
import jax.numpy as jnp
from jax import lax
from jax.experimental import pallas as pl
from jax.experimental.pallas import tpu as pltpu

F32 = jnp.float32
BF16 = jnp.bfloat16

D_MODEL = 1024
DEPTH = 2
CHUNK = 64
CHUNK_LOG2 = 6
SGU_WIDTH = 512
SGU_GROUPS = 4
GROUP_DIM = 128
DN_WIDTH = 512
DN_HEADS = 4
HEAD_DIM = 128
CONV_W = 4
CONV_CH = 3 * DN_WIDTH
D_FF = 4 * D_MODEL
EPS = 1e-6

O_UV = 0
O_QKV = 2 * SGU_WIDTH
O_Z = O_QKV + CONV_CH
O_AB = O_Z + DN_WIDTH
IN_WIDTH = O_AB + 2 * DN_HEADS
AB_LANES = 128
G_LANE0 = DN_HEADS
CONV_PAD = 8

VMEM_LIMIT_BYTES = 56 * 1024 * 1024


def _sigmoid(x):
    return 0.5 * (1.0 + jnp.tanh(0.5 * x))


def _silu(x):
    return x * _sigmoid(x)


def _softplus(x):
    return jnp.maximum(x, 0.0) + jnp.log(1.0 + jnp.exp(-jnp.abs(x)))


def _gelu_tanh(x):
    c = 0.7978845608028654
    return 0.5 * x * (1.0 + jnp.tanh(c * (x + 0.044715 * (x * x * x))))


def _dot(a, b):
    return jnp.dot(a, b, preferred_element_type=F32)


def _dot_nt(a, b):
    return lax.dot_general(a, b, (((1,), (1,)), ((), ())), preferred_element_type=F32)


def _split3(x):
    hi = x.astype(BF16)
    r = x - hi.astype(F32)
    mid = r.astype(BF16)
    lo = (r - mid.astype(F32)).astype(BF16)
    return hi, mid, lo


def _blockdiag_lanes128(y):
    yb = y.astype(BF16)
    z = jnp.zeros((y.shape[0], HEAD_DIM), BF16)
    top = jnp.concatenate([yb[:, :HEAD_DIM], z], axis=1)
    bot = jnp.concatenate([z, yb[:, HEAD_DIM:]], axis=1)
    return jnp.concatenate([top, bot], axis=0)


def _layer_spec(shape, layer):
    return pl.BlockSpec((None,) + tuple(shape), lambda i, j: (layer,) + (0,) * len(shape))


MOD_BLOCK = 1024


def _mod_kernel(c_ref, w_ref, b_ref, o_ref):
    cs = _silu(c_ref[...]).astype(BF16)
    o_ref[...] = _dot(cs, w_ref[...].astype(BF16)) + b_ref[...]


def _modulation(c_all, ada_w, ada_b):
    n = c_all.shape[0]
    width = ada_w.shape[2]
    return pl.pallas_call(
        _mod_kernel,
        grid=(DEPTH, width // MOD_BLOCK),
        in_specs=[
            pl.BlockSpec((n, D_MODEL), lambda l, j: (0, 0)),
            pl.BlockSpec((None, D_MODEL, MOD_BLOCK), lambda l, j: (l, 0, j)),
            pl.BlockSpec((None, 1, MOD_BLOCK), lambda l, j: (l, 0, j)),
        ],
        out_specs=pl.BlockSpec((None, n, MOD_BLOCK), lambda l, j: (l, 0, j)),
        out_shape=jax.ShapeDtypeStruct((DEPTH, n, width), F32),
        compiler_params=pltpu.CompilerParams(
            dimension_semantics=("arbitrary", "arbitrary"), vmem_limit_bytes=VMEM_LIMIT_BYTES),
        name="adaln_modulation",
    )(c_all, ada_w, ada_b.reshape(DEPTH, 1, width))


def _mixer_kernel(*refs, bg, p, pv, emit_v, zero_init):
    refs = list(refs)
    x_ref, mod_ref = refs[:2]
    refs = refs[2:]
    if not zero_init:
        conv0_ref, delta0_ref = refs[:2]
        refs = refs[2:]
    (ng_ref, win_ref, wab_ref, sgng_ref, sgw_ref, sgb_ref, cw_ref, dtb_ref, alog_ref, dng_ref, wout_ref,
     xo_ref, convo_ref, deltao_ref) = refs[:14]
    refs = refs[14:]
    if emit_v:
        vo_ref = refs[0]
        refs = refs[1:]
    ctail, cbuf, act, abl, zs, mixb = refs
    rows = bg * p
    nchunk = p // CHUNK
    step = pl.program_id(1)

    @pl.when(step == 0)
    def _():
        if zero_init:
            ctail[...] = jnp.zeros(ctail.shape, F32)
            deltao_ref[...] = jnp.zeros(deltao_ref.shape, F32)
        else:
            ctail[...] = conv0_ref[...]
            deltao_ref[...] = delta0_ref[...]

    x3 = x_ref[...]
    ms = jnp.mean(x3 * x3, axis=-1, keepdims=True)
    mod = mod_ref[...]
    h3 = (x3 * lax.rsqrt(ms + EPS)) * ng_ref[...] * (1.0 + mod[:, 1:2, :]) + mod[:, 0:1, :]
    if pv < p:
        h3 = jnp.concatenate([h3, jnp.zeros((bg, p - pv, D_MODEL), F32)], axis=1)
    hb = h3.reshape(rows, D_MODEL).astype(BF16)

    uv = _gelu_tanh(_dot(hb, win_ref[:, O_UV:O_QKV]))
    u = uv[:, :SGU_WIDTH]
    v = uv[:, SGU_WIDTH:]
    vms = jnp.mean(v * v, axis=-1, keepdims=True)
    vn = (v * lax.rsqrt(vms + EPS)) * sgng_ref[...]
    if emit_v:
        vo_ref[...] = vn.reshape(bg, p, SGU_WIDTH)[:, :pv, :]
    ri = lax.broadcasted_iota(jnp.int32, (p, p), 0)
    ci = lax.broadcasted_iota(jnp.int32, (p, p), 1)
    sg_mask = (ci >> CHUNK_LOG2) <= (ri >> CHUNK_LOG2)
    vnb = vn.astype(BF16)
    for g in range(SGU_GROUPS):
        wg = jnp.where(sg_mask, sgw_ref[g], 0.0).astype(BF16)
        bias = sgb_ref[g]
        lo, hi = g * GROUP_DIM, (g + 1) * GROUP_DIM
        for b in range(bg):
            r0 = b * p
            s = _dot(wg, vnb[r0:r0 + p, lo:hi]) + bias
            mixb[r0:r0 + p, lo:hi] = (u[r0:r0 + p, lo:hi] * s).astype(BF16)

    cbuf[:, 0:CONV_PAD, :] = ctail[...]
    cbuf[:, CONV_PAD:CONV_PAD + p, :] = _dot(hb, win_ref[:, O_QKV:O_Z]).reshape(bg, p, CONV_CH)
    cw = cw_ref[...]
    y = cbuf[:, CONV_PAD:CONV_PAD + p, :] * cw[CONV_W - 1:CONV_W, :]
    for i in range(1, CONV_W):
        y = y + cbuf[:, CONV_PAD - i:CONV_PAD - i + p, :] * cw[CONV_W - 1 - i:CONV_W - i, :]
    ctail[...] = cbuf[:, pv:pv + CONV_PAD, :]
    convo_ref[...] = cbuf[:, pv + CONV_PAD - (CONV_W - 1):pv + CONV_PAD, :]
    a3 = _silu(y)
    act[...] = a3.reshape(rows, CONV_CH)
    for hh in range(2 * DN_HEADS):
        lo, hi = hh * HEAD_DIM, (hh + 1) * HEAD_DIM
        t = act[:, lo:hi]
        scale = HEAD_DIM ** -0.5 if hh < DN_HEADS else 1.0
        act[:, lo:hi] = t * (lax.rsqrt(jnp.sum(t * t, axis=-1, keepdims=True) + EPS) * scale)
    zs[...] = _silu(_dot(hb, win_ref[:, O_Z:O_AB]))
    abl[...] = _dot(hb, wab_ref[...])

    ii = lax.broadcasted_iota(jnp.int32, (CHUNK, 2 * HEAD_DIM), 0)
    lane4 = lax.broadcasted_iota(jnp.int32, (CHUNK, 2 * HEAD_DIM), 1)
    jj = lane4 & (CHUNK - 1)
    incl = ii >= jj
    strict = ii > jj
    upper = ii <= jj
    eye4 = jnp.where(ii == jj, 1.0, 0.0).astype(F32)
    head_of_lane = lane4 >> CHUNK_LOG2
    left2 = lax.broadcasted_iota(jnp.int32, (CHUNK, HEAD_DIM), 1) < CHUNK
    kk = lax.broadcasted_iota(jnp.int32, (CHUNK, 3 * CHUNK), 1) & (CHUNK - 1)
    rr = lax.broadcasted_iota(jnp.int32, (CHUNK, 3 * CHUNK), 0)
    ltri3 = jnp.where(kk <= rr, 1.0, 0.0).astype(BF16)
    ones3 = jnp.ones((CHUNK, 3 * CHUNK), BF16)
    level_masks = []
    for k in range(CHUNK_LOG2):
        level_masks.append(jnp.where(((ii ^ jj) >> k) == 1, (ii >> k) & 1, 0) == 1)
    dtb = dtb_ref[...]
    neg_decay = -jnp.exp(alog_ref[...])
    dng = dng_ref[...]
    pos = lax.broadcasted_iota(jnp.int32, (CHUNK, AB_LANES), 0)

    def lanes(t, lane0):
        return jnp.concatenate(
            [jnp.broadcast_to(t[:, lane0 + hh:lane0 + hh + 1], (CHUNK, HEAD_DIM)) for hh in range(DN_HEADS)], axis=1)

    def pack4(t):
        c0 = jnp.where(left2, t[:, 0:128], t[:, 128:256])
        c1 = jnp.where(left2, t[:, 256:384], t[:, 384:512])
        return jnp.concatenate([c0, c1], axis=1)

    def blockdiag4(y):
        return jnp.concatenate([jnp.where(head_of_lane == hh, y, 0.0) for hh in range(DN_HEADS)],
                               axis=0).astype(BF16)

    pre = {}
    for b in range(bg):
        for c in range(nchunk):
            r0 = b * p + c * CHUNK
            ab = abl[r0:r0 + CHUNK, :]
            beta_c = _sigmoid(ab)
            g_c = neg_decay * _softplus(ab + dtb)
            if pv < p:
                valid = (pos + c * CHUNK) < pv
                beta_c = jnp.where(valid, beta_c, 0.0)
                g_c = jnp.where(valid, g_c, 0.0)
            gc_c = _dot(ltri3, jnp.concatenate(_split3(g_c), axis=0))
            beta = lanes(beta_c, 0)
            gc = lanes(gc_c, G_LANE0)
            g4 = pack4(lanes(g_c, G_LANE0))
            p4 = _split3(jnp.where(upper, g4, 0.0))
            grow4 = _dot(ones3, jnp.concatenate(p4, axis=0))
            diff4 = pack4(gc) - grow4
            dec_incl = jnp.exp(jnp.where(incl, diff4, -jnp.inf))
            dec_strict = jnp.where(strict, dec_incl, 0.0)
            egc = jnp.exp(gc)
            glast = gc[CHUNK - 1:CHUNK, :]
            ekg = jnp.exp(glast - gc)
            qn = act[r0:r0 + CHUNK, 0:DN_WIDTH]
            kn = act[r0:r0 + CHUNK, DN_WIDTH:2 * DN_WIDTH]
            vv = act[r0:r0 + CHUNK, 2 * DN_WIDTH:3 * DN_WIDTH]
            qb = qn.astype(BF16)
            kb = kn.astype(BF16)
            qk_cols, kk_cols = [], []
            for pr in range(DN_HEADS // 2):
                lo, hi = pr * 2 * HEAD_DIM, (pr + 1) * 2 * HEAD_DIM
                lhs = jnp.concatenate([qb[:, lo:hi], kb[:, lo:hi]], axis=0)
                r = _dot_nt(lhs, _blockdiag_lanes128(kn[:, lo:hi]))
                qk_cols.append(r[:CHUNK])
                kk_cols.append(r[CHUNK:])
            attn4 = jnp.concatenate(qk_cols, axis=1) * dec_incl
            a4 = jnp.concatenate(kk_cols, axis=1) * (pack4(beta) * dec_strict)
            pre[(b, c)] = dict(a4=a4, attn4=attn4.astype(BF16), beta=beta, vv=vv,
                               kgc=(kn * egc).astype(BF16), qg=(qn * egc).astype(BF16),
                               kg=kn * ekg, gl=jnp.exp(glast))

    probs = [(b, c) for b in range(bg) for c in range(nchunk)]
    a4 = {key: pre[key]["a4"] for key in probs}
    dinv = {key: eye4 - jnp.where(level_masks[0], a4[key], 0.0) for key in probs}
    for k in range(1, CHUNK_LOG2):
        mk = level_masks[k]
        p1 = {key: _dot(dinv[key].astype(BF16), blockdiag4(jnp.where(mk, a4[key], 0.0))) for key in probs}
        p2 = {key: _dot(p1[key].astype(BF16), blockdiag4(dinv[key])) for key in probs}
        dinv = {key: dinv[key] - p2[key] for key in probs}

    pairs = [(b, pr) for b in range(bg) for pr in range(DN_HEADS // 2)]
    for c in range(nchunk):
        s_pair, r, v_new, yy = {}, {}, {}, {}
        for (b, pr) in pairs:
            pc = pre[(b, c)]
            lo, hi = pr * 2 * HEAD_DIM, (pr + 1) * 2 * HEAD_DIM
            s_pair[b, pr] = jnp.concatenate([deltao_ref[b, 2 * pr], deltao_ref[b, 2 * pr + 1]], axis=1)
            lhs = jnp.concatenate([pc["kgc"][:, lo:hi], pc["qg"][:, lo:hi]], axis=0)
            r[b, pr] = _dot(lhs, _blockdiag_lanes128(s_pair[b, pr]))
        for (b, pr) in pairs:
            pc = pre[(b, c)]
            lo, hi = pr * 2 * HEAD_DIM, (pr + 1) * 2 * HEAD_DIM
            xr = pc["beta"][:, lo:hi] * (pc["vv"][:, lo:hi] - r[b, pr][:CHUNK])
            t_pair = dinv[(b, c)][:, pr * HEAD_DIM:(pr + 1) * HEAD_DIM].astype(BF16)
            v_new[b, pr] = _dot(t_pair, _blockdiag_lanes128(xr))
        for (b, pr) in pairs:
            pc = pre[(b, c)]
            lo, hi = pr * 2 * HEAD_DIM, (pr + 1) * 2 * HEAD_DIM
            kg_pair = jnp.concatenate([pc["kg"][:, lo:lo + HEAD_DIM], pc["kg"][:, lo + HEAD_DIM:hi]],
                                      axis=0)
            lhs2 = jnp.concatenate([pc["attn4"][:, pr * HEAD_DIM:(pr + 1) * HEAD_DIM],
                                    kg_pair.T.astype(BF16)], axis=0)
            yy[b, pr] = _dot(lhs2, _blockdiag_lanes128(v_new[b, pr]))
        for (b, pr) in pairs:
            lo, hi = pr * 2 * HEAD_DIM, (pr + 1) * 2 * HEAD_DIM
            s_new = s_pair[b, pr] * pre[(b, c)]["gl"][:, lo:hi] + yy[b, pr][CHUNK:]
            deltao_ref[b, 2 * pr] = s_new[:, :HEAD_DIM]
            deltao_ref[b, 2 * pr + 1] = s_new[:, HEAD_DIM:]
        for b in range(bg):
            r0 = b * p + c * CHUNK
            for hh in range(DN_HEADS):
                lo = (hh % 2) * HEAD_DIM
                o_h = r[b, hh // 2][CHUNK:, lo:lo + HEAD_DIM] + yy[b, hh // 2][:CHUNK, lo:lo + HEAD_DIM]
                oms = jnp.mean(o_h * o_h, axis=-1, keepdims=True)
                gated = (o_h * lax.rsqrt(oms + EPS)) * dng * zs[r0:r0 + CHUNK, hh * HEAD_DIM:(hh + 1) * HEAD_DIM]
                mixb[r0:r0 + CHUNK, SGU_WIDTH + hh * HEAD_DIM:SGU_WIDTH + (hh + 1) * HEAD_DIM] = gated.astype(BF16)

    out = _dot(mixb[...], wout_ref[...]).reshape(bg, p, D_MODEL)
    xo_ref[...] = x_ref[...] + mod[:, 2:3, :] * out[:, :pv, :]


def _mixer(x, mod4, conv_in, delta_in, prm, sgw, sgb, *, layer, mod_off, bg, p, pv, emit_v):
    batch, seq, _ = x.shape
    zero_init = conv_in is None
    grid = (batch // bg, seq // pv)
    rows = bg * p
    mod_blk = mod_off // bg
    in_specs = [
        pl.BlockSpec((bg, pv, D_MODEL), lambda i, j: (i, j, 0)),
        pl.BlockSpec((None, bg, 6, D_MODEL), lambda i, j: (layer, i + mod_blk, 0, 0)),
    ]
    args = [x, mod4]
    if not zero_init:
        in_specs += [
            pl.BlockSpec((None, bg, CONV_PAD, CONV_CH), lambda i, j: (layer, i, 0, 0)),
            pl.BlockSpec((None, bg, DN_HEADS, HEAD_DIM, HEAD_DIM), lambda i, j: (layer, i, 0, 0, 0)),
        ]
        args += [conv_in, delta_in]
    in_specs += [
        _layer_spec((1, D_MODEL), layer), _layer_spec((D_MODEL, IN_WIDTH), layer),
        _layer_spec((D_MODEL, AB_LANES), layer),
        _layer_spec((1, SGU_WIDTH), layer), _layer_spec((SGU_GROUPS, p, p), layer),
        _layer_spec((SGU_GROUPS, p, GROUP_DIM), layer), _layer_spec((CONV_W, CONV_CH), layer),
        _layer_spec((1, AB_LANES), layer), _layer_spec((1, AB_LANES), layer),
        _layer_spec((1, HEAD_DIM), layer), _layer_spec((D_MODEL, D_MODEL), layer),
    ]
    args += [prm["ng_mix"], prm["win"], prm["wab"], prm["sgng"], sgw, sgb, prm["cw"], prm["dtb"], prm["alog"],
             prm["dng"], prm["wout"]]
    out_specs = [
        pl.BlockSpec((bg, pv, D_MODEL), lambda i, j: (i, j, 0)),
        pl.BlockSpec((bg, CONV_W - 1, CONV_CH), lambda i, j: (i, 0, 0)),
        pl.BlockSpec((bg, DN_HEADS, HEAD_DIM, HEAD_DIM), lambda i, j: (i, 0, 0, 0)),
    ]
    out_shape = [
        jax.ShapeDtypeStruct((batch, seq, D_MODEL), F32),
        jax.ShapeDtypeStruct((batch, CONV_W - 1, CONV_CH), F32),
        jax.ShapeDtypeStruct((batch, DN_HEADS, HEAD_DIM, HEAD_DIM), F32),
    ]
    if emit_v:
        out_specs.append(pl.BlockSpec((bg, pv, SGU_WIDTH), lambda i, j: (i, j, 0)))
        out_shape.append(jax.ShapeDtypeStruct((batch, seq, SGU_WIDTH), F32))
    scratch = [
        pltpu.VMEM((bg, CONV_PAD, CONV_CH), F32),
        pltpu.VMEM((bg, CONV_PAD + p, CONV_CH), F32),
        pltpu.VMEM((rows, CONV_CH), F32),
        pltpu.VMEM((rows, AB_LANES), F32),
        pltpu.VMEM((rows, DN_WIDTH), F32),
        pltpu.VMEM((rows, D_MODEL), BF16),
    ]
    return pl.pallas_call(
        functools.partial(_mixer_kernel, bg=bg, p=p, pv=pv, emit_v=emit_v, zero_init=zero_init),
        grid=grid, in_specs=in_specs, out_specs=out_specs, out_shape=out_shape,
        scratch_shapes=scratch,
        compiler_params=pltpu.CompilerParams(
            dimension_semantics=("arbitrary", "arbitrary"), vmem_limit_bytes=VMEM_LIMIT_BYTES),
        name="mixer",
    )(*args)


def _ffn_kernel(x_ref, mod_ref, ng_ref, wup_ref, wdn_ref, fg_ref, o_ref, *, bb, tm, final_norm):
    x3 = x_ref[...]
    mod = mod_ref[...]
    ms = jnp.mean(x3 * x3, axis=-1, keepdims=True)
    h3 = (x3 * lax.rsqrt(ms + EPS)) * ng_ref[...] * (1.0 + mod[:, 4:5, :]) + mod[:, 3:4, :]
    hb = h3.reshape(bb * tm, D_MODEL).astype(BF16)
    a = jnp.maximum(_dot(hb, wup_ref[...]), 0.0)
    a = (a * a).astype(BF16)
    y = x3 + mod[:, 5:6, :] * _dot(a, wdn_ref[...]).reshape(bb, tm, D_MODEL)
    if final_norm:
        yms = jnp.mean(y * y, axis=-1, keepdims=True)
        y = (y * lax.rsqrt(yms + EPS)) * fg_ref[...]
    o_ref[...] = y


def _ffn(x, mod4, prm, *, layer, mod_off, bb, tm, final_norm):
    batch, seq, _ = x.shape
    mod_blk = mod_off // bb
    return pl.pallas_call(
        functools.partial(_ffn_kernel, bb=bb, tm=tm, final_norm=final_norm),
        grid=(batch // bb, seq // tm),
        in_specs=[
            pl.BlockSpec((bb, tm, D_MODEL), lambda i, j: (i, j, 0)),
            pl.BlockSpec((None, bb, 6, D_MODEL), lambda i, j: (layer, i + mod_blk, 0, 0)),
            _layer_spec((1, D_MODEL), layer), _layer_spec((D_MODEL, D_FF), layer),
            _layer_spec((D_FF, D_MODEL), layer),
            pl.BlockSpec((1, D_MODEL), lambda i, j: (0, 0)),
        ],
        out_specs=pl.BlockSpec((bb, tm, D_MODEL), lambda i, j: (i, j, 0)),
        out_shape=jax.ShapeDtypeStruct((batch, seq, D_MODEL), F32),
        compiler_params=pltpu.CompilerParams(
            dimension_semantics=("arbitrary", "arbitrary"), vmem_limit_bytes=VMEM_LIMIT_BYTES),
        name="ffn",
    )(x, mod4, prm["ng_ffn"], prm["wup"], prm["wdn"], prm["fg"])


def _trunk(x, mod4, conv_in, delta_in, prm, sgw, sgb, *, mod_off, bg, p, pv, ffn_bb, ffn_tm, emit_v):
    convs, deltas, vrows = [], [], []
    for l in range(DEPTH):
        outs = _mixer(x, mod4, conv_in, delta_in, prm, sgw, sgb, layer=l, mod_off=mod_off,
                      bg=bg, p=p, pv=pv, emit_v=emit_v)
        convs.append(outs[1])
        deltas.append(outs[2])
        if emit_v:
            vrows.append(outs[3])
        x = _ffn(outs[0], mod4, prm, layer=l, mod_off=mod_off, bb=ffn_bb, tm=ffn_tm,
                 final_norm=(l == DEPTH - 1))
    return x, jnp.stack(convs), jnp.stack(deltas), (jnp.stack(vrows) if emit_v else None)


def kernel(x_prompt, x_sample, c_prompt, c_sample, state_conv, state_delta, ada_w, ada_b, norm_mix_g,
           norm_ffn_g, w_in, sgu_norm_g, sgu_w, sgu_b, conv_w, dt_bias, a_log, dn_norm_g, w_out, w_up,
           w_down, final_norm_g):
    nb, seq, _ = x_prompt.shape
    ns, dec_seq, _ = x_sample.shape
    sgu_chunk = sgu_w.shape[-1]

    head_lanes = ((0, 0), (G_LANE0, AB_LANES - 2 * DN_HEADS))
    prm = dict(
        ng_mix=norm_mix_g[:, None, :], ng_ffn=norm_ffn_g[:, None, :], sgng=sgu_norm_g[:, None, :],
        win=w_in.astype(BF16),
        wab=jnp.pad(w_in[:, :, O_AB:], ((0, 0), (0, 0), (0, AB_LANES - 2 * DN_HEADS))).astype(BF16),
        cw=conv_w, dtb=jnp.pad(dt_bias, head_lanes)[:, None, :],
        alog=jnp.pad(a_log, head_lanes)[:, None, :], dng=dn_norm_g[:, None, :],
        wout=w_out.astype(BF16), wup=w_up.astype(BF16), wdn=w_down.astype(BF16), fg=final_norm_g[None, :])
    sgb_p = jnp.broadcast_to(sgu_b[:, :, :, None], (DEPTH, SGU_GROUPS, sgu_chunk, GROUP_DIM))
    pad = CHUNK - dec_seq
    sgw_s = jnp.pad(sgu_w[:, :, :dec_seq, :dec_seq], ((0, 0), (0, 0), (0, pad), (0, pad)))
    sgb_s = jnp.pad(sgb_p[:, :, :dec_seq, :], ((0, 0), (0, 0), (0, pad), (0, 0)))
    conv_in = jnp.pad(state_conv, ((0, 0), (0, 0), (CONV_PAD - (CONV_W - 1), 0), (0, 0)))

    mod_all = _modulation(jnp.concatenate([c_prompt, c_sample], axis=0), ada_w, ada_b)
    mod4 = mod_all.reshape(DEPTH, nb + ns, 6, D_MODEL)

    y_prompt, prompt_conv, prompt_delta, _ = _trunk(
        x_prompt, mod4, None, None, prm, sgu_w, sgb_p,
        mod_off=0, bg=4, p=sgu_chunk, pv=sgu_chunk, ffn_bb=1, ffn_tm=512, emit_v=False)

    y_sample, sample_conv, sample_delta, sample_v = _trunk(
        x_sample, mod4, conv_in, state_delta, prm, sgw_s, sgb_s,
        mod_off=nb, bg=ns, p=CHUNK, pv=dec_seq, ffn_bb=ns, ffn_tm=dec_seq, emit_v=True)

    return (y_prompt, y_sample, prompt_conv, prompt_delta, sample_conv, sample_delta, sample_v)
```

```python
import functools

import jax
import jax.numpy as jnp
from jax import lax
from jax.experimental import pallas as pl
from jax.experimental.pallas import tpu as pltpu

F32 = jnp.float32
BF16 = jnp.bfloat16

D_MODEL = 1024
DEPTH = 2
CHUNK = 64
CHUNK_LOG2 = 6
SGU_WIDTH = 512
SGU_GROUPS = 4
GROUP_DIM = 128
DN_WIDTH = 512
DN_HEADS = 4
HEAD_DIM = 128
CONV_W = 4
CONV_CH = 3 * DN_WIDTH
D_FF = 4 * D_MODEL
EPS = 1e-6

O_UV = 0
O_QKV = 2 * SGU_WIDTH
O_Z = O_QKV + CONV_CH
O_AB = O_Z + DN_WIDTH
AB_LANES = 128
IN_PAD = O_AB + AB_LANES
G_LANE0 = DN_HEADS
CONV_PAD = 8

VMEM_LIMIT_BYTES = 56 * 1024 * 1024
MXU_COLS = 256


def _sigmoid(x):
    return 0.5 * (1.0 + jnp.tanh(0.5 * x))


def _silu(x):
    return x * _sigmoid(x)


def _softplus(x):
    return jnp.maximum(x, 0.0) + jnp.log(1.0 + jnp.exp(-jnp.abs(x)))


def _gelu_tanh(x):
    c = 0.7978845608028654
    return 0.5 * x * (1.0 + jnp.tanh(c * (x + 0.044715 * (x * x * x))))


def _dot(a, b):
    return jnp.dot(a, b, preferred_element_type=F32)


def _dot_nt(a, b):
    return lax.dot_general(a, b, (((1,), (1,)), ((), ())), preferred_element_type=F32)


def _split3(x):
    hi = x.astype(BF16)
    r = x - hi.astype(F32)
    mid = r.astype(BF16)
    lo = (r - mid.astype(F32)).astype(BF16)
    return hi, mid, lo


def _blockdiag_lanes128(y):
    yb = y.astype(BF16)
    z = jnp.zeros((y.shape[0], HEAD_DIM), BF16)
    top = jnp.concatenate([yb[:, :HEAD_DIM], z], axis=1)
    bot = jnp.concatenate([z, yb[:, HEAD_DIM:]], axis=1)
    return jnp.concatenate([top, bot], axis=0)


def _interleave(groups):
    keyed = [((i + 0.5) / len(g), gi, i, f) for gi, g in enumerate(groups) for i, f in enumerate(g)]
    return [f for _, _, _, f in sorted(keyed, key=lambda e: e[:3])]


def _layer_spec(shape, layer):
    return pl.BlockSpec((None,) + tuple(shape), lambda i, j: (layer,) + (0,) * len(shape))


MOD_BLOCK = 1024


def _mod_kernel(c_ref, w_ref, b_ref, o_ref):
    cs = _silu(c_ref[...]).astype(BF16)
    o_ref[...] = _dot(cs, w_ref[...].astype(BF16)) + b_ref[...]


def _modulation(c_all, ada_w, ada_b):
    n = c_all.shape[0]
    width = ada_w.shape[2]
    return pl.pallas_call(
        _mod_kernel,
        grid=(DEPTH, width // MOD_BLOCK),
        in_specs=[
            pl.BlockSpec((n, D_MODEL), lambda l, j: (0, 0)),
            pl.BlockSpec((None, D_MODEL, MOD_BLOCK), lambda l, j: (l, 0, j)),
            pl.BlockSpec((None, 1, MOD_BLOCK), lambda l, j: (l, 0, j)),
        ],
        out_specs=pl.BlockSpec((None, n, MOD_BLOCK), lambda l, j: (l, 0, j)),
        out_shape=jax.ShapeDtypeStruct((DEPTH, n, width), F32),
        compiler_params=pltpu.CompilerParams(
            dimension_semantics=("arbitrary", "arbitrary"), vmem_limit_bytes=VMEM_LIMIT_BYTES),
        name="adaln_modulation",
    )(c_all, ada_w, ada_b.reshape(DEPTH, 1, width))


def _mixer_kernel(*refs, bg, p, pv, emit_v, zero_init):
    refs = list(refs)
    x_ref, mod_ref = refs[:2]
    refs = refs[2:]
    if not zero_init:
        conv0_ref, delta0_ref = refs[:2]
        refs = refs[2:]
    (ng_ref, win_ref, sgng_ref, sgw_ref, sgb_ref, cw_ref, dtb_ref, alog_ref, dng_ref, wout_ref,
     xo_ref, convo_ref, deltao_ref) = refs[:13]
    refs = refs[13:]
    if emit_v:
        vo_ref = refs[0]
        refs = refs[1:]
    ctail, cbuf, act, abl, zs, mixb = refs
    nchunk = p // CHUNK
    step = pl.program_id(1)

    @pl.when(step == 0)
    def _():
        if zero_init:
            ctail[...] = jnp.zeros(ctail.shape, F32)
            deltao_ref[...] = jnp.zeros(deltao_ref.shape, F32)
        else:
            ctail[...] = conv0_ref[...]
            deltao_ref[...] = delta0_ref[...]

    nhalf = 2 if bg % 2 == 0 else 1
    hbg = bg // nhalf
    hrows = hbg * p
    mod = mod_ref[...]
    cw = cw_ref[...]
    ri = lax.broadcasted_iota(jnp.int32, (p, p), 0)
    ci = lax.broadcasted_iota(jnp.int32, (p, p), 1)
    sg_mask = (ci >> CHUNK_LOG2) <= (ri >> CHUNK_LOG2)
    sg_w = [jnp.where(sg_mask, sgw_ref[g], 0.0).astype(BF16) for g in range(SGU_GROUPS)]
    state = [dict() for _ in range(nhalf)]
    pre = {}

    def stage_norm(h):
        st, b0 = state[h], h * hbg

        def run():
            x3 = x_ref[b0:b0 + hbg]
            ms = jnp.mean(x3 * x3, axis=-1, keepdims=True)
            h3 = ((x3 * lax.rsqrt(ms + EPS)) * ng_ref[...] * (1.0 + mod[b0:b0 + hbg, 1:2, :])
                  + mod[b0:b0 + hbg, 0:1, :])
            if pv < p:
                h3 = jnp.concatenate([h3, jnp.zeros((hbg, p - pv, D_MODEL), F32)], axis=1)
            st["hb"] = h3.reshape(hrows, D_MODEL).astype(BF16)
        return [run]

    def stage_uv(h):
        st = state[h]
        st["uv"] = [None] * (O_QKV // MXU_COLS)

        def piece(n):
            def run():
                st["uv"][n] = _dot(st["hb"], win_ref[:, O_UV + n * MXU_COLS:O_UV + (n + 1) * MXU_COLS])
            return run
        return [piece(n) for n in range(O_QKV // MXU_COLS)]

    def stage_gate(h):
        st, b0 = state[h], h * hbg
        nblk = O_QKV // MXU_COLS

        def act_piece(n):
            def run():
                st["uv"][n] = _gelu_tanh(st["uv"][n])
            return run

        def vnorm():
            v = jnp.concatenate(st["uv"][nblk // 2:], axis=1)
            vms = jnp.mean(v * v, axis=-1, keepdims=True)
            vn = (v * lax.rsqrt(vms + EPS)) * sgng_ref[...]
            if emit_v:
                vo_ref[b0:b0 + hbg] = vn.reshape(hbg, p, SGU_WIDTH)[:, :pv, :]
            st["vnb"] = vn.astype(BF16)
            st["u"] = jnp.concatenate(st["uv"][:nblk // 2], axis=1)

        def gate_piece(g, b):
            def run():
                lo, hi = g * GROUP_DIM, (g + 1) * GROUP_DIM
                r0 = b * p
                s = _dot(sg_w[g], st["vnb"][r0:r0 + p, lo:hi]) + sgb_ref[g]
                mixb[(b0 + b) * p:(b0 + b + 1) * p, lo:hi] = (st["u"][r0:r0 + p, lo:hi] * s).astype(BF16)
            return run
        return ([act_piece(n) for n in range(nblk)] + [vnorm]
                + [gate_piece(g, b) for g in range(SGU_GROUPS) for b in range(hbg)])

    def stage_qkv(h):
        st, b0 = state[h], h * hbg

        def piece(n):
            def run():
                if n == 0:
                    cbuf[b0:b0 + hbg, 0:CONV_PAD, :] = ctail[b0:b0 + hbg]
                lo, hi = n * MXU_COLS, (n + 1) * MXU_COLS
                cbuf[b0:b0 + hbg, CONV_PAD:CONV_PAD + p, lo:hi] = _dot(
                    st["hb"], win_ref[:, O_QKV + lo:O_QKV + hi]).reshape(hbg, p, MXU_COLS)
            return run
        return [piece(n) for n in range(CONV_CH // MXU_COLS)]

    def stage_conv(h):
        b0 = h * hbg
        r0 = b0 * p

        def piece(n):
            def run():
                lo, hi = n * MXU_COLS, (n + 1) * MXU_COLS
                y = cbuf[b0:b0 + hbg, CONV_PAD:CONV_PAD + p, lo:hi] * cw[CONV_W - 1:CONV_W, lo:hi]
                for i in range(1, CONV_W):
                    y = y + (cbuf[b0:b0 + hbg, CONV_PAD - i:CONV_PAD - i + p, lo:hi]
                             * cw[CONV_W - 1 - i:CONV_W - i, lo:hi])
                a = _silu(y).reshape(hrows, MXU_COLS)
                for lo_h in range(lo, hi, HEAD_DIM):
                    t = a[:, lo_h - lo:lo_h - lo + HEAD_DIM]
                    if lo_h < 2 * DN_WIDTH:
                        scale = HEAD_DIM ** -0.5 if lo_h < DN_WIDTH else 1.0
                        t = t * (lax.rsqrt(jnp.sum(t * t, axis=-1, keepdims=True) + EPS) * scale)
                    act[r0:r0 + hrows, lo_h:lo_h + HEAD_DIM] = t
            return run

        def tail():
            ctail[b0:b0 + hbg] = cbuf[b0:b0 + hbg, pv:pv + CONV_PAD, :]
            convo_ref[b0:b0 + hbg] = cbuf[b0:b0 + hbg, pv + CONV_PAD - (CONV_W - 1):pv + CONV_PAD, :]
        return [piece(n) for n in range(CONV_CH // MXU_COLS)] + [tail]

    def stage_zab(h):
        st = state[h]
        r0 = h * hrows

        def z_piece(n):
            def run():
                lo, hi = n * MXU_COLS, (n + 1) * MXU_COLS
                zs[r0:r0 + hrows, lo:hi] = _silu(_dot(st["hb"], win_ref[:, O_Z + lo:O_Z + hi]))
            return run

        def ab_piece():
            abl[r0:r0 + hrows, :] = _dot(st["hb"], win_ref[:, O_AB:IN_PAD])
        return [z_piece(n) for n in range(DN_WIDTH // MXU_COLS)] + [ab_piece]

    ii = lax.broadcasted_iota(jnp.int32, (CHUNK, 2 * HEAD_DIM), 0)
    lane4 = lax.broadcasted_iota(jnp.int32, (CHUNK, 2 * HEAD_DIM), 1)
    jj = lane4 & (CHUNK - 1)
    incl = ii >= jj
    strict = ii > jj
    eye4 = jnp.where(ii == jj, 1.0, 0.0).astype(F32)
    head_of_lane = lane4 >> CHUNK_LOG2
    left2 = lax.broadcasted_iota(jnp.int32, (CHUNK, HEAD_DIM), 1) < CHUNK
    kk = lax.broadcasted_iota(jnp.int32, (CHUNK, 3 * CHUNK), 1) & (CHUNK - 1)
    rr = lax.broadcasted_iota(jnp.int32, (CHUNK, 3 * CHUNK), 0)
    ltri3 = jnp.where(kk <= rr, 1.0, 0.0).astype(BF16)
    level_masks = []
    for k in range(CHUNK_LOG2):
        level_masks.append(jnp.where(((ii ^ jj) >> k) == 1, (ii >> k) & 1, 0) == 1)
    dtb = dtb_ref[...]
    neg_decay = -jnp.exp(alog_ref[...])
    dng = dng_ref[...]
    pos = lax.broadcasted_iota(jnp.int32, (CHUNK, AB_LANES), 0)

    def lanes(t, lane0):
        return jnp.concatenate(
            [jnp.broadcast_to(t[:, lane0 + hh:lane0 + hh + 1], (CHUNK, HEAD_DIM)) for hh in range(DN_HEADS)], axis=1)

    def pack4(t):
        c0 = jnp.where(left2, t[:, 0:128], t[:, 128:256])
        c1 = jnp.where(left2, t[:, 256:384], t[:, 384:512])
        return jnp.concatenate([c0, c1], axis=1)

    head_sel = [jnp.where(head_of_lane == hh, 1.0, 0.0).astype(BF16) for hh in range(DN_HEADS)]
    level_sel = [[jnp.where(head_of_lane == hh, jnp.where(level_masks[k], 1.0, 0.0), 0.0).astype(BF16)
                  for hh in range(DN_HEADS)] for k in range(CHUNK_LOG2)]

    def blockdiag4(yb, sel):
        return jnp.concatenate([yb * sel[hh] for hh in range(DN_HEADS)], axis=0)

    def stage_pre(h):
        def piece(b, c):
            def run():
                r0 = b * p + c * CHUNK
                ab = abl[r0:r0 + CHUNK, :]
                beta_c = _sigmoid(ab)
                g_c = neg_decay * _softplus(ab + dtb)
                if pv < p:
                    valid = (pos + c * CHUNK) < pv
                    beta_c = jnp.where(valid, beta_c, 0.0)
                    g_c = jnp.where(valid, g_c, 0.0)
                gc_c = _dot(ltri3, jnp.concatenate(_split3(g_c), axis=0))
                glast_c = gc_c[CHUNK - 1:CHUNK, :]
                beta = lanes(beta_c, 0)
                egc = lanes(jnp.exp(gc_c), G_LANE0)
                ekg = lanes(jnp.exp(glast_c - gc_c), G_LANE0)
                gl = jnp.concatenate([jnp.broadcast_to(jnp.exp(glast_c)[:, G_LANE0 + hh:G_LANE0 + hh + 1],
                                                       (1, HEAD_DIM)) for hh in range(DN_HEADS)], axis=1)
                gc_t = gc_c.T
                grow4 = jnp.broadcast_to(
                    jnp.concatenate([gc_t[G_LANE0 + hh:G_LANE0 + hh + 1, :] for hh in range(DN_HEADS)], axis=1),
                    (CHUNK, DN_HEADS * CHUNK))
                diff4 = pack4(lanes(gc_c, G_LANE0)) - grow4
                dec_incl = jnp.exp(jnp.where(incl, diff4, -jnp.inf))
                dec_strict = jnp.where(strict, dec_incl, 0.0)
                qn = act[r0:r0 + CHUNK, 0:DN_WIDTH]
                kn = act[r0:r0 + CHUNK, DN_WIDTH:2 * DN_WIDTH]
                vv = act[r0:r0 + CHUNK, 2 * DN_WIDTH:3 * DN_WIDTH]
                qb = qn.astype(BF16)
                kb = kn.astype(BF16)
                qk_cols, kk_cols = [], []
                for pr in range(DN_HEADS // 2):
                    lo, hi = pr * 2 * HEAD_DIM, (pr + 1) * 2 * HEAD_DIM
                    lhs = jnp.concatenate([qb[:, lo:hi], kb[:, lo:hi]], axis=0)
                    r = _dot_nt(lhs, _blockdiag_lanes128(kn[:, lo:hi]))
                    qk_cols.append(r[:CHUNK])
                    kk_cols.append(r[CHUNK:])
                attn4 = jnp.concatenate(qk_cols, axis=1) * dec_incl
                a4 = jnp.concatenate(kk_cols, axis=1) * (pack4(beta) * dec_strict)
                pre[(b, c)] = dict(a4=a4, attn4=attn4.astype(BF16), beta=beta, vv=vv,
                                   kgc=(kn * egc).astype(BF16), qg=(qn * egc).astype(BF16),
                                   kg=kn * ekg, gl=gl)
            return run
        return [piece(b, c) for b in range(h * hbg, (h + 1) * hbg) for c in range(nchunk)]

    stages = [stage_norm, stage_uv, stage_gate, stage_qkv, stage_conv, stage_zab, stage_pre]
    for t in range(len(stages) + nhalf - 1):
        active = [stages[t - h](h) for h in range(nhalf) if 0 <= t - h < len(stages)]
        for run in _interleave(active):
            run()

    probs = [(b, c) for b in range(bg) for c in range(nchunk)]
    a4 = {key: pre[key]["a4"] for key in probs}
    a4b = {key: a4[key].astype(BF16) for key in probs}
    dinv = {key: eye4 - jnp.where(level_masks[0], a4[key], 0.0) for key in probs}
    for k in range(1, CHUNK_LOG2):
        db = {key: dinv[key].astype(BF16) for key in probs}
        p1 = {key: _dot(db[key], blockdiag4(a4b[key], level_sel[k])) for key in probs}
        p2 = {key: _dot(p1[key].astype(BF16), blockdiag4(db[key], head_sel)) for key in probs}
        dinv = {key: dinv[key] - p2[key] for key in probs}

    pairs = [(b, pr) for b in range(bg) for pr in range(DN_HEADS // 2)]
    for c in range(nchunk):
        s_pair, r, v_new, yy = {}, {}, {}, {}
        for (b, pr) in pairs:
            pc = pre[(b, c)]
            lo, hi = pr * 2 * HEAD_DIM, (pr + 1) * 2 * HEAD_DIM
            s_pair[b, pr] = jnp.concatenate([deltao_ref[b, 2 * pr], deltao_ref[b, 2 * pr + 1]], axis=1)
            lhs = jnp.concatenate([pc["kgc"][:, lo:hi], pc["qg"][:, lo:hi]], axis=0)
            r[b, pr] = _dot(lhs, _blockdiag_lanes128(s_pair[b, pr]))
        for (b, pr) in pairs:
            pc = pre[(b, c)]
            lo, hi = pr * 2 * HEAD_DIM, (pr + 1) * 2 * HEAD_DIM
            xr = pc["beta"][:, lo:hi] * (pc["vv"][:, lo:hi] - r[b, pr][:CHUNK])
            t_pair = dinv[(b, c)][:, pr * HEAD_DIM:(pr + 1) * HEAD_DIM].astype(BF16)
            v_new[b, pr] = _dot(t_pair, _blockdiag_lanes128(xr))
        for (b, pr) in pairs:
            pc = pre[(b, c)]
            lo, hi = pr * 2 * HEAD_DIM, (pr + 1) * 2 * HEAD_DIM
            kg_pair = jnp.concatenate([pc["kg"][:, lo:lo + HEAD_DIM], pc["kg"][:, lo + HEAD_DIM:hi]],
                                      axis=0)
            lhs2 = jnp.concatenate([pc["attn4"][:, pr * HEAD_DIM:(pr + 1) * HEAD_DIM],
                                    kg_pair.T.astype(BF16)], axis=0)
            yy[b, pr] = _dot(lhs2, _blockdiag_lanes128(v_new[b, pr]))
        for (b, pr) in pairs:
            lo, hi = pr * 2 * HEAD_DIM, (pr + 1) * 2 * HEAD_DIM
            s_new = s_pair[b, pr] * pre[(b, c)]["gl"][:, lo:hi] + yy[b, pr][CHUNK:]
            deltao_ref[b, 2 * pr] = s_new[:, :HEAD_DIM]
            deltao_ref[b, 2 * pr + 1] = s_new[:, HEAD_DIM:]
        for b in range(bg):
            r0 = b * p + c * CHUNK
            for hh in range(DN_HEADS):
                lo = (hh % 2) * HEAD_DIM
                o_h = r[b, hh // 2][CHUNK:, lo:lo + HEAD_DIM] + yy[b, hh // 2][:CHUNK, lo:lo + HEAD_DIM]
                oms = jnp.mean(o_h * o_h, axis=-1, keepdims=True)
                gated = (o_h * lax.rsqrt(oms + EPS)) * dng * zs[r0:r0 + CHUNK, hh * HEAD_DIM:(hh + 1) * HEAD_DIM]
                mixb[r0:r0 + CHUNK, SGU_WIDTH + hh * HEAD_DIM:SGU_WIDTH + (hh + 1) * HEAD_DIM] = gated.astype(BF16)

    out = _dot(mixb[...], wout_ref[...]).reshape(bg, p, D_MODEL)
    xo_ref[...] = x_ref[...] + mod[:, 2:3, :] * out[:, :pv, :]


def _mixer(x, mod4, conv_in, delta_in, prm, sgw, sgb, *, layer, mod_off, bg, p, pv, emit_v):
    batch, seq, _ = x.shape
    zero_init = conv_in is None
    grid = (batch // bg, seq // pv)
    rows = bg * p
    mod_blk = mod_off // bg
    in_specs = [
        pl.BlockSpec((bg, pv, D_MODEL), lambda i, j: (i, j, 0)),
        pl.BlockSpec((None, bg, 6, D_MODEL), lambda i, j: (layer, i + mod_blk, 0, 0)),
    ]
    args = [x, mod4]
    if not zero_init:
        in_specs += [
            pl.BlockSpec((None, bg, CONV_PAD, CONV_CH), lambda i, j: (layer, i, 0, 0)),
            pl.BlockSpec((None, bg, DN_HEADS, HEAD_DIM, HEAD_DIM), lambda i, j: (layer, i, 0, 0, 0)),
        ]
        args += [conv_in, delta_in]
    in_specs += [
        _layer_spec((1, D_MODEL), layer), _layer_spec((D_MODEL, IN_PAD), layer),
        _layer_spec((1, SGU_WIDTH), layer), _layer_spec((SGU_GROUPS, p, p), layer),
        _layer_spec((SGU_GROUPS, p, GROUP_DIM), layer), _layer_spec((CONV_W, CONV_CH), layer),
        _layer_spec((1, AB_LANES), layer), _layer_spec((1, AB_LANES), layer),
        _layer_spec((1, HEAD_DIM), layer), _layer_spec((D_MODEL, D_MODEL), layer),
    ]
    args += [prm["ng_mix"], prm["win"], prm["sgng"], sgw, sgb, prm["cw"], prm["dtb"], prm["alog"],
             prm["dng"], prm["wout"]]
    out_specs = [
        pl.BlockSpec((bg, pv, D_MODEL), lambda i, j: (i, j, 0)),
        pl.BlockSpec((bg, CONV_W - 1, CONV_CH), lambda i, j: (i, 0, 0)),
        pl.BlockSpec((bg, DN_HEADS, HEAD_DIM, HEAD_DIM), lambda i, j: (i, 0, 0, 0)),
    ]
    out_shape = [
        jax.ShapeDtypeStruct((batch, seq, D_MODEL), F32),
        jax.ShapeDtypeStruct((batch, CONV_W - 1, CONV_CH), F32),
        jax.ShapeDtypeStruct((batch, DN_HEADS, HEAD_DIM, HEAD_DIM), F32),
    ]
    if emit_v:
        out_specs.append(pl.BlockSpec((bg, pv, SGU_WIDTH), lambda i, j: (i, j, 0)))
        out_shape.append(jax.ShapeDtypeStruct((batch, seq, SGU_WIDTH), F32))
    scratch = [
        pltpu.VMEM((bg, CONV_PAD, CONV_CH), F32),
        pltpu.VMEM((bg, CONV_PAD + p, CONV_CH), F32),
        pltpu.VMEM((rows, CONV_CH), F32),
        pltpu.VMEM((rows, AB_LANES), F32),
        pltpu.VMEM((rows, DN_WIDTH), F32),
        pltpu.VMEM((rows, D_MODEL), BF16),
    ]
    return pl.pallas_call(
        functools.partial(_mixer_kernel, bg=bg, p=p, pv=pv, emit_v=emit_v, zero_init=zero_init),
        grid=grid, in_specs=in_specs, out_specs=out_specs, out_shape=out_shape,
        scratch_shapes=scratch,
        compiler_params=pltpu.CompilerParams(
            dimension_semantics=("arbitrary", "arbitrary"), vmem_limit_bytes=VMEM_LIMIT_BYTES),
        name="mixer",
    )(*args)


def _ffn_kernel(x_ref, mod_ref, ng_ref, wup_ref, wdn_ref, fg_ref, o_ref, *, bb, tm, final_norm):
    x3 = x_ref[...]
    mod = mod_ref[...]
    ms = jnp.mean(x3 * x3, axis=-1, keepdims=True)
    h3 = (x3 * lax.rsqrt(ms + EPS)) * ng_ref[...] * (1.0 + mod[:, 4:5, :]) + mod[:, 3:4, :]
    hb = h3.reshape(bb * tm, D_MODEL).astype(BF16)
    a = jnp.maximum(_dot(hb, wup_ref[...]), 0.0)
    a = (a * a).astype(BF16)
    y = x3 + mod[:, 5:6, :] * _dot(a, wdn_ref[...]).reshape(bb, tm, D_MODEL)
    if final_norm:
        yms = jnp.mean(y * y, axis=-1, keepdims=True)
        y = (y * lax.rsqrt(yms + EPS)) * fg_ref[...]
    o_ref[...] = y


def _ffn(x, mod4, prm, *, layer, mod_off, bb, tm, final_norm):
    batch, seq, _ = x.shape
    mod_blk = mod_off // bb
    return pl.pallas_call(
        functools.partial(_ffn_kernel, bb=bb, tm=tm, final_norm=final_norm),
        grid=(batch // bb, seq // tm),
        in_specs=[
            pl.BlockSpec((bb, tm, D_MODEL), lambda i, j: (i, j, 0)),
            pl.BlockSpec((None, bb, 6, D_MODEL), lambda i, j: (layer, i + mod_blk, 0, 0)),
            _layer_spec((1, D_MODEL), layer), _layer_spec((D_MODEL, D_FF), layer),
            _layer_spec((D_FF, D_MODEL), layer),
            pl.BlockSpec((1, D_MODEL), lambda i, j: (0, 0)),
        ],
        out_specs=pl.BlockSpec((bb, tm, D_MODEL), lambda i, j: (i, j, 0)),
        out_shape=jax.ShapeDtypeStruct((batch, seq, D_MODEL), F32),
        compiler_params=pltpu.CompilerParams(
            dimension_semantics=("arbitrary", "arbitrary"), vmem_limit_bytes=VMEM_LIMIT_BYTES),
        name="ffn",
    )(x, mod4, prm["ng_ffn"], prm["wup"], prm["wdn"], prm["fg"])


def _trunk(x, mod4, conv_in, delta_in, prm, sgw, sgb, *, mod_off, bg, p, pv, ffn_bb, ffn_tm, emit_v):
    convs, deltas, vrows = [], [], []
    for l in range(DEPTH):
        outs = _mixer(x, mod4, conv_in, delta_in, prm, sgw, sgb, layer=l, mod_off=mod_off,
                      bg=bg, p=p, pv=pv, emit_v=emit_v)
        convs.append(outs[1])
        deltas.append(outs[2])
        if emit_v:
            vrows.append(outs[3])
        x = _ffn(outs[0], mod4, prm, layer=l, mod_off=mod_off, bb=ffn_bb, tm=ffn_tm,
                 final_norm=(l == DEPTH - 1))
    return x, jnp.stack(convs), jnp.stack(deltas), (jnp.stack(vrows) if emit_v else None)


def kernel(x_prompt, x_sample, c_prompt, c_sample, state_conv, state_delta, ada_w, ada_b, norm_mix_g,
           norm_ffn_g, w_in, sgu_norm_g, sgu_w, sgu_b, conv_w, dt_bias, a_log, dn_norm_g, w_out, w_up,
           w_down, final_norm_g):
    nb, seq, _ = x_prompt.shape
    ns, dec_seq, _ = x_sample.shape
    sgu_chunk = sgu_w.shape[-1]

    head_lanes = ((0, 0), (G_LANE0, AB_LANES - 2 * DN_HEADS))
    prm = dict(
        ng_mix=norm_mix_g[:, None, :], ng_ffn=norm_ffn_g[:, None, :], sgng=sgu_norm_g[:, None, :],
        win=jnp.pad(w_in, ((0, 0), (0, 0), (0, AB_LANES - 2 * DN_HEADS))).astype(BF16),
        cw=conv_w, dtb=jnp.pad(dt_bias, head_lanes)[:, None, :],
        alog=jnp.pad(a_log, head_lanes)[:, None, :], dng=dn_norm_g[:, None, :],
        wout=w_out.astype(BF16), wup=w_up.astype(BF16), wdn=w_down.astype(BF16), fg=final_norm_g[None, :])
    sgb_p = jnp.broadcast_to(sgu_b[:, :, :, None], (DEPTH, SGU_GROUPS, sgu_chunk, GROUP_DIM))
    pad = CHUNK - dec_seq
    sgw_s = jnp.pad(sgu_w[:, :, :dec_seq, :dec_seq], ((0, 0), (0, 0), (0, pad), (0, pad)))
    sgb_s = jnp.pad(sgb_p[:, :, :dec_seq, :], ((0, 0), (0, 0), (0, pad), (0, 0)))
    conv_in = jnp.pad(state_conv, ((0, 0), (0, 0), (CONV_PAD - (CONV_W - 1), 0), (0, 0)))

    mod_all = _modulation(jnp.concatenate([c_prompt, c_sample], axis=0), ada_w, ada_b)
    mod4 = mod_all.reshape(DEPTH, nb + ns, 6, D_MODEL)

    y_prompt, prompt_conv, prompt_delta, _ = _trunk(
        x_prompt, mod4, None, None, prm, sgu_w, sgb_p,
        mod_off=0, bg=4, p=sgu_chunk, pv=sgu_chunk, ffn_bb=1, ffn_tm=512, emit_v=False)

    y_sample, sample_conv, sample_delta, sample_v = _trunk(
        x_sample, mod4, conv_in, state_delta, prm, sgw_s, sgb_s,
        mod_off=nb, bg=ns, p=CHUNK, pv=dec_seq, ffn_bb=ns, ffn_tm=dec_seq, emit_v=True)

    return (y_prompt, y_sample, prompt_conv, prompt_delta, sample_conv, sample_delta, sample_v)
```

```python
import functools

import jax
import jax.numpy as jnp
from jax import lax
from jax.experimental import pallas as pl
from jax.experimental.pallas import tpu as pltpu

F32 = jnp.float32
BF16 = jnp.bfloat16

D_MODEL = 1024
DEPTH = 2
CHUNK = 64
CHUNK_LOG2 = 6
SGU_WIDTH = 512
SGU_GROUPS = 4
GROUP_DIM = 128
DN_WIDTH = 512
DN_HEADS = 4
HEAD_DIM = 128
CONV_W = 4
CONV_CH = 3 * DN_WIDTH
D_FF = 4 * D_MODEL
EPS = 1e-6

O_UV = 0
O_QKV = 2 * SGU_WIDTH
O_Z = O_QKV + CONV_CH
O_AB = O_Z + DN_WIDTH
AB_LANES = 128
G_LANE0 = DN_HEADS
CONV_PAD = 8

VMEM_LIMIT_BYTES = 56 * 1024 * 1024
MXU_COLS = 256


def _sigmoid(x):
    return 0.5 * (1.0 + jnp.tanh(0.5 * x))


def _silu(x):
    h = 0.5 * x
    return h + h * jnp.tanh(h)


def _softplus(x):
    return jnp.maximum(x, 0.0) + jnp.log(1.0 + jnp.exp(-jnp.abs(x)))


def _gelu_tanh(x):
    c = 0.7978845608028654
    h = 0.5 * x
    return h + h * jnp.tanh(x * (c + (c * 0.044715) * (x * x)))


def _dot(a, b):
    return jnp.dot(a, b, preferred_element_type=F32)


def _dot_nt(a, b):
    return lax.dot_general(a, b, (((1,), (1,)), ((), ())), preferred_element_type=F32)


def _split3(x):
    hi = x.astype(BF16)
    r = x - hi.astype(F32)
    mid = r.astype(BF16)
    lo = (r - mid.astype(F32)).astype(BF16)
    return hi, mid, lo


def _blockdiag_lanes128(y):
    yb = y.astype(BF16)
    z = jnp.zeros((y.shape[0], HEAD_DIM), BF16)
    top = jnp.concatenate([yb[:, :HEAD_DIM], z], axis=1)
    bot = jnp.concatenate([z, yb[:, HEAD_DIM:]], axis=1)
    return jnp.concatenate([top, bot], axis=0)


def _interleave(groups):
    keyed = [((i + 0.5) / len(g), gi, i, f) for gi, g in enumerate(groups) for i, f in enumerate(g)]
    return [f for _, _, _, f in sorted(keyed, key=lambda e: e[:3])]


def _layer_spec(shape, layer):
    return pl.BlockSpec((None,) + tuple(shape), lambda i, j: (layer,) + (0,) * len(shape))


MOD_BLOCK = 2048


def _mod_kernel(c_ref, w_ref, b_ref, o_ref):
    cs = _silu(c_ref[...]).astype(BF16)
    o_ref[...] = _dot(cs, w_ref[...].astype(BF16)) + b_ref[...]


def _modulation(c_all, ada_w, ada_b):
    n = c_all.shape[0]
    width = ada_w.shape[2]
    return pl.pallas_call(
        _mod_kernel,
        grid=(DEPTH, width // MOD_BLOCK),
        in_specs=[
            pl.BlockSpec((n, D_MODEL), lambda l, j: (0, 0)),
            pl.BlockSpec((None, D_MODEL, MOD_BLOCK), lambda l, j: (l, 0, j)),
            pl.BlockSpec((None, 1, MOD_BLOCK), lambda l, j: (l, 0, j)),
        ],
        out_specs=pl.BlockSpec((None, n, MOD_BLOCK), lambda l, j: (l, 0, j)),
        out_shape=jax.ShapeDtypeStruct((DEPTH, n, width), F32),
        compiler_params=pltpu.CompilerParams(
            dimension_semantics=("arbitrary", "arbitrary"), vmem_limit_bytes=VMEM_LIMIT_BYTES),
        name="adaln_modulation",
    )(c_all, ada_w, ada_b.reshape(DEPTH, 1, width))


def _mixer_kernel(*refs, bg, p, pv, emit_v, zero_init):
    refs = list(refs)
    x_ref, mod_ref = refs[:2]
    refs = refs[2:]
    if not zero_init:
        conv0_ref, delta0_ref = refs[:2]
        refs = refs[2:]
    (ng_ref, win_ref, wab_ref, sgng_ref, sgw_ref, sgb_ref, cw_ref, dtb_ref, alog_ref, dng_ref, wout_ref,
     xo_ref, convo_ref, deltao_ref) = refs[:14]
    refs = refs[14:]
    if emit_v:
        vo_ref = refs[0]
        refs = refs[1:]
    ctail, cbuf, act, abl, zs, mixb = refs
    nchunk = p // CHUNK
    step = pl.program_id(1)

    @pl.when(step == 0)
    def _():
        if zero_init:
            ctail[...] = jnp.zeros(ctail.shape, F32)
            deltao_ref[...] = jnp.zeros(deltao_ref.shape, F32)
        else:
            ctail[...] = conv0_ref[...]
            deltao_ref[...] = delta0_ref[...]

    nhalf = 2 if bg % 2 == 0 else 1
    hbg = bg // nhalf
    hrows = hbg * p
    mod = mod_ref[...]
    cw = cw_ref[...]
    ri = lax.broadcasted_iota(jnp.int32, (p, p), 0)
    ci = lax.broadcasted_iota(jnp.int32, (p, p), 1)
    sg_mask = (ci >> CHUNK_LOG2) <= (ri >> CHUNK_LOG2)
    sg_w = [jnp.where(sg_mask, sgw_ref[g], 0.0).astype(BF16) for g in range(SGU_GROUPS)]
    state = [dict() for _ in range(nhalf)]
    pre = {}

    def stage_norm(h):
        st, b0 = state[h], h * hbg

        def run():
            x3 = x_ref[b0:b0 + hbg]
            ms = jnp.mean(x3 * x3, axis=-1, keepdims=True)
            gain = ng_ref[...] * (1.0 + mod[b0:b0 + hbg, 1:2, :])
            h3 = (x3 * lax.rsqrt(ms + EPS)) * gain + mod[b0:b0 + hbg, 0:1, :]
            if pv < p:
                h3 = jnp.concatenate([h3, jnp.zeros((hbg, p - pv, D_MODEL), F32)], axis=1)
            st["hb"] = h3.reshape(hrows, D_MODEL).astype(BF16)
        return [run]

    def stage_uv(h):
        st = state[h]
        st["uv"] = [None] * (O_QKV // MXU_COLS)

        def piece(n):
            def run():
                st["uv"][n] = _dot(st["hb"], win_ref[:, O_UV + n * MXU_COLS:O_UV + (n + 1) * MXU_COLS])
            return run
        return [piece(n) for n in range(O_QKV // MXU_COLS)]

    def stage_gate(h):
        st, b0 = state[h], h * hbg
        nblk = O_QKV // MXU_COLS

        def act_piece(n):
            def run():
                st["uv"][n] = _gelu_tanh(st["uv"][n])
            return run

        def vnorm():
            v = jnp.concatenate(st["uv"][nblk // 2:], axis=1)
            vms = jnp.mean(v * v, axis=-1, keepdims=True)
            vn = (v * lax.rsqrt(vms + EPS)) * sgng_ref[...]
            if emit_v:
                vo_ref[b0:b0 + hbg] = vn.reshape(hbg, p, SGU_WIDTH)[:, :pv, :]
            st["vnb"] = vn.astype(BF16)
            st["u"] = jnp.concatenate(st["uv"][:nblk // 2], axis=1)

        def gate_piece(g, b):
            def run():
                lo, hi = g * GROUP_DIM, (g + 1) * GROUP_DIM
                r0 = b * p
                s = _dot(sg_w[g], st["vnb"][r0:r0 + p, lo:hi]) + sgb_ref[g]
                mixb[(b0 + b) * p:(b0 + b + 1) * p, lo:hi] = (st["u"][r0:r0 + p, lo:hi] * s).astype(BF16)
            return run
        return ([act_piece(n) for n in range(nblk)] + [vnorm]
                + [gate_piece(g, b) for g in range(SGU_GROUPS) for b in range(hbg)])

    def stage_qkv(h):
        st, b0 = state[h], h * hbg

        def piece(n):
            def run():
                if n == 0:
                    cbuf[b0:b0 + hbg, 0:CONV_PAD, :] = ctail[b0:b0 + hbg]
                lo, hi = n * MXU_COLS, (n + 1) * MXU_COLS
                cbuf[b0:b0 + hbg, CONV_PAD:CONV_PAD + p, lo:hi] = _dot(
                    st["hb"], win_ref[:, O_QKV + lo:O_QKV + hi]).reshape(hbg, p, MXU_COLS)
            return run
        return [piece(n) for n in range(CONV_CH // MXU_COLS)]

    def stage_conv(h):
        b0 = h * hbg
        r0 = b0 * p

        def piece(n):
            def run():
                lo, hi = n * MXU_COLS, (n + 1) * MXU_COLS
                y = cbuf[b0:b0 + hbg, CONV_PAD:CONV_PAD + p, lo:hi] * cw[CONV_W - 1:CONV_W, lo:hi]
                for i in range(1, CONV_W):
                    y = y + (cbuf[b0:b0 + hbg, CONV_PAD - i:CONV_PAD - i + p, lo:hi]
                             * cw[CONV_W - 1 - i:CONV_W - i, lo:hi])
                a = _silu(y).reshape(hrows, MXU_COLS)
                for lo_h in range(lo, hi, HEAD_DIM):
                    t = a[:, lo_h - lo:lo_h - lo + HEAD_DIM]
                    if lo_h < 2 * DN_WIDTH:
                        scale = HEAD_DIM ** -0.5 if lo_h < DN_WIDTH else 1.0
                        t = t * (lax.rsqrt(jnp.sum(t * t, axis=-1, keepdims=True) + EPS) * scale)
                    act[r0:r0 + hrows, lo_h:lo_h + HEAD_DIM] = t
            return run

        def tail():
            ctail[b0:b0 + hbg] = cbuf[b0:b0 + hbg, pv:pv + CONV_PAD, :]
            convo_ref[b0:b0 + hbg] = cbuf[b0:b0 + hbg, pv + CONV_PAD - (CONV_W - 1):pv + CONV_PAD, :]
        return [piece(n) for n in range(CONV_CH // MXU_COLS)] + [tail]

    def stage_zab(h):
        st = state[h]
        r0 = h * hrows

        def z_piece(n):
            def run():
                lo, hi = n * MXU_COLS, (n + 1) * MXU_COLS
                zs[r0:r0 + hrows, lo:hi] = _silu(_dot(st["hb"], win_ref[:, O_Z + lo:O_Z + hi]))
            return run

        def ab_piece():
            abl[r0:r0 + hrows, :] = _dot(st["hb"], wab_ref[...])
        return [z_piece(n) for n in range(DN_WIDTH // MXU_COLS)] + [ab_piece]

    ii = lax.broadcasted_iota(jnp.int32, (CHUNK, 2 * HEAD_DIM), 0)
    lane4 = lax.broadcasted_iota(jnp.int32, (CHUNK, 2 * HEAD_DIM), 1)
    jj = lane4 & (CHUNK - 1)
    incl = ii >= jj
    strict = ii > jj
    eye4 = jnp.where(ii == jj, 1.0, 0.0).astype(F32)
    head_of_lane = lane4 >> CHUNK_LOG2
    left2 = lax.broadcasted_iota(jnp.int32, (CHUNK, HEAD_DIM), 1) < CHUNK
    kk = lax.broadcasted_iota(jnp.int32, (CHUNK, 3 * CHUNK), 1) & (CHUNK - 1)
    rr = lax.broadcasted_iota(jnp.int32, (CHUNK, 3 * CHUNK), 0)
    ltri3 = jnp.where(kk <= rr, 1.0, 0.0).astype(BF16)
    level_masks = []
    for k in range(CHUNK_LOG2):
        level_masks.append(jnp.where(((ii ^ jj) >> k) == 1, (ii >> k) & 1, 0) == 1)
    dtb = dtb_ref[...]
    neg_decay = -jnp.exp(alog_ref[...])
    dng = dng_ref[...]
    pos = lax.broadcasted_iota(jnp.int32, (CHUNK, AB_LANES), 0)

    def lanes(t, lane0):
        return jnp.concatenate(
            [jnp.broadcast_to(t[:, lane0 + hh:lane0 + hh + 1], (CHUNK, HEAD_DIM)) for hh in range(DN_HEADS)], axis=1)

    def pack4(t):
        c0 = jnp.where(left2, t[:, 0:128], t[:, 128:256])
        c1 = jnp.where(left2, t[:, 256:384], t[:, 384:512])
        return jnp.concatenate([c0, c1], axis=1)

    head_sel = [jnp.where(head_of_lane == hh, 1.0, 0.0).astype(BF16) for hh in range(DN_HEADS)]
    level_sel = [[jnp.where(head_of_lane == hh, jnp.where(level_masks[k], 1.0, 0.0), 0.0).astype(BF16)
                  for hh in range(DN_HEADS)] for k in range(CHUNK_LOG2)]

    def blockdiag4(yb, sel):
        return jnp.concatenate([yb * sel[hh] for hh in range(DN_HEADS)], axis=0)

    def stage_pre(h):
        def piece(b, c):
            def run():
                r0 = b * p + c * CHUNK
                ab = abl[r0:r0 + CHUNK, :]
                beta_c = _sigmoid(ab)
                g_c = neg_decay * _softplus(ab + dtb)
                if pv < p:
                    valid = (pos + c * CHUNK) < pv
                    beta_c = jnp.where(valid, beta_c, 0.0)
                    g_c = jnp.where(valid, g_c, 0.0)
                gc_c = _dot(ltri3, jnp.concatenate(_split3(g_c), axis=0))
                glast_c = gc_c[CHUNK - 1:CHUNK, :]
                beta = lanes(beta_c, 0)
                egc = lanes(jnp.exp(gc_c), G_LANE0)
                ekg = lanes(jnp.exp(glast_c - gc_c), G_LANE0)
                gl = jnp.concatenate([jnp.broadcast_to(jnp.exp(glast_c)[:, G_LANE0 + hh:G_LANE0 + hh + 1],
                                                       (1, HEAD_DIM)) for hh in range(DN_HEADS)], axis=1)
                gc_t = gc_c.T
                grow4 = jnp.broadcast_to(
                    jnp.concatenate([gc_t[G_LANE0 + hh:G_LANE0 + hh + 1, :] for hh in range(DN_HEADS)], axis=1),
                    (CHUNK, DN_HEADS * CHUNK))
                diff4 = pack4(lanes(gc_c, G_LANE0)) - grow4
                dec_incl = jnp.exp(jnp.where(incl, diff4, -jnp.inf))
                dec_strict = jnp.where(strict, dec_incl, 0.0)
                qn = act[r0:r0 + CHUNK, 0:DN_WIDTH]
                kn = act[r0:r0 + CHUNK, DN_WIDTH:2 * DN_WIDTH]
                vv = act[r0:r0 + CHUNK, 2 * DN_WIDTH:3 * DN_WIDTH]
                qb = qn.astype(BF16)
                kb = kn.astype(BF16)
                qk_cols, kk_cols = [], []
                for pr in range(DN_HEADS // 2):
                    lo, hi = pr * 2 * HEAD_DIM, (pr + 1) * 2 * HEAD_DIM
                    lhs = jnp.concatenate([qb[:, lo:hi], kb[:, lo:hi]], axis=0)
                    r = _dot_nt(lhs, _blockdiag_lanes128(kn[:, lo:hi]))
                    qk_cols.append(r[:CHUNK])
                    kk_cols.append(r[CHUNK:])
                attn4 = jnp.concatenate(qk_cols, axis=1) * dec_incl
                a4 = jnp.concatenate(kk_cols, axis=1) * (pack4(beta) * dec_strict)
                pre[(b, c)] = dict(a4=a4, attn4=attn4.astype(BF16), beta=beta, vv=vv,
                                   kgc=(kn * egc).astype(BF16), qg=(qn * egc).astype(BF16),
                                   kg=kn * ekg, gl=gl)
            return run
        return [piece(b, c) for b in range(h * hbg, (h + 1) * hbg) for c in range(nchunk)]

    stages = [stage_norm, stage_uv, stage_gate, stage_qkv, stage_conv, stage_zab, stage_pre]
    for t in range(len(stages) + nhalf - 1):
        active = [stages[t - h](h) for h in range(nhalf) if 0 <= t - h < len(stages)]
        for run in _interleave(active):
            run()

    probs = [(b, c) for b in range(bg) for c in range(nchunk)]
    a4 = {key: pre[key]["a4"] for key in probs}
    a4b = {key: a4[key].astype(BF16) for key in probs}
    dinv = {key: eye4 - jnp.where(level_masks[0], a4[key], 0.0) for key in probs}
    for k in range(1, CHUNK_LOG2):
        db = {key: dinv[key].astype(BF16) for key in probs}
        p1 = {key: _dot(db[key], blockdiag4(a4b[key], level_sel[k])) for key in probs}
        p2 = {key: _dot(p1[key].astype(BF16), blockdiag4(db[key], head_sel)) for key in probs}
        dinv = {key: dinv[key] - p2[key] for key in probs}

    pairs = [(b, pr) for b in range(bg) for pr in range(DN_HEADS // 2)]
    for c in range(nchunk):
        s_pair, r, v_new, yy = {}, {}, {}, {}
        for (b, pr) in pairs:
            pc = pre[(b, c)]
            lo, hi = pr * 2 * HEAD_DIM, (pr + 1) * 2 * HEAD_DIM
            s_pair[b, pr] = jnp.concatenate([deltao_ref[b, 2 * pr], deltao_ref[b, 2 * pr + 1]], axis=1)
            lhs = jnp.concatenate([pc["kgc"][:, lo:hi], pc["qg"][:, lo:hi]], axis=0)
            r[b, pr] = _dot(lhs, _blockdiag_lanes128(s_pair[b, pr]))
        for (b, pr) in pairs:
            pc = pre[(b, c)]
            lo, hi = pr * 2 * HEAD_DIM, (pr + 1) * 2 * HEAD_DIM
            xr = pc["beta"][:, lo:hi] * (pc["vv"][:, lo:hi] - r[b, pr][:CHUNK])
            t_pair = dinv[(b, c)][:, pr * HEAD_DIM:(pr + 1) * HEAD_DIM].astype(BF16)
            v_new[b, pr] = _dot(t_pair, _blockdiag_lanes128(xr))
        for (b, pr) in pairs:
            pc = pre[(b, c)]
            lo, hi = pr * 2 * HEAD_DIM, (pr + 1) * 2 * HEAD_DIM
            kg_pair = jnp.concatenate([pc["kg"][:, lo:lo + HEAD_DIM], pc["kg"][:, lo + HEAD_DIM:hi]],
                                      axis=0)
            lhs2 = jnp.concatenate([pc["attn4"][:, pr * HEAD_DIM:(pr + 1) * HEAD_DIM],
                                    kg_pair.T.astype(BF16)], axis=0)
            yy[b, pr] = _dot(lhs2, _blockdiag_lanes128(v_new[b, pr]))
        for (b, pr) in pairs:
            lo, hi = pr * 2 * HEAD_DIM, (pr + 1) * 2 * HEAD_DIM
            s_new = s_pair[b, pr] * pre[(b, c)]["gl"][:, lo:hi] + yy[b, pr][CHUNK:]
            deltao_ref[b, 2 * pr] = s_new[:, :HEAD_DIM]
            deltao_ref[b, 2 * pr + 1] = s_new[:, HEAD_DIM:]
        for b in range(bg):
            r0 = b * p + c * CHUNK
            for hh in range(DN_HEADS):
                lo = (hh % 2) * HEAD_DIM
                o_h = r[b, hh // 2][CHUNK:, lo:lo + HEAD_DIM] + yy[b, hh // 2][:CHUNK, lo:lo + HEAD_DIM]
                oms = jnp.mean(o_h * o_h, axis=-1, keepdims=True)
                gated = (o_h * lax.rsqrt(oms + EPS)) * dng * zs[r0:r0 + CHUNK, hh * HEAD_DIM:(hh + 1) * HEAD_DIM]
                mixb[r0:r0 + CHUNK, SGU_WIDTH + hh * HEAD_DIM:SGU_WIDTH + (hh + 1) * HEAD_DIM] = gated.astype(BF16)

    out = _dot(mixb[...], wout_ref[...]).reshape(bg, p, D_MODEL)
    xo_ref[...] = x_ref[...] + mod[:, 2:3, :] * out[:, :pv, :]


def _mixer(x, mod4, conv_in, delta_in, prm, sgw, sgb, *, layer, mod_off, bg, p, pv, emit_v):
    batch, seq, _ = x.shape
    zero_init = conv_in is None
    grid = (batch // bg, seq // pv)
    rows = bg * p
    mod_blk = mod_off // bg
    in_specs = [
        pl.BlockSpec((bg, pv, D_MODEL), lambda i, j: (i, j, 0)),
        pl.BlockSpec((None, bg, 6, D_MODEL), lambda i, j: (layer, i + mod_blk, 0, 0)),
    ]
    args = [x, mod4]
    if not zero_init:
        in_specs += [
            pl.BlockSpec((None, bg, CONV_PAD, CONV_CH), lambda i, j: (layer, i, 0, 0)),
            pl.BlockSpec((None, bg, DN_HEADS, HEAD_DIM, HEAD_DIM), lambda i, j: (layer, i, 0, 0, 0)),
        ]
        args += [conv_in, delta_in]
    in_specs += [
        _layer_spec((1, D_MODEL), layer), _layer_spec((D_MODEL, O_AB), layer),
        _layer_spec((D_MODEL, AB_LANES), layer),
        _layer_spec((1, SGU_WIDTH), layer), _layer_spec((SGU_GROUPS, p, p), layer),
        _layer_spec((SGU_GROUPS, p, GROUP_DIM), layer), _layer_spec((CONV_W, CONV_CH), layer),
        _layer_spec((1, AB_LANES), layer), _layer_spec((1, AB_LANES), layer),
        _layer_spec((1, HEAD_DIM), layer), _layer_spec((D_MODEL, D_MODEL), layer),
    ]
    args += [prm["ng_mix"], prm["win"], prm["wab"], prm["sgng"], sgw, sgb, prm["cw"], prm["dtb"], prm["alog"],
             prm["dng"], prm["wout"]]
    out_specs = [
        pl.BlockSpec((bg, pv, D_MODEL), lambda i, j: (i, j, 0)),
        pl.BlockSpec((bg, CONV_W - 1, CONV_CH), lambda i, j: (i, 0, 0)),
        pl.BlockSpec((bg, DN_HEADS, HEAD_DIM, HEAD_DIM), lambda i, j: (i, 0, 0, 0)),
    ]
    out_shape = [
        jax.ShapeDtypeStruct((batch, seq, D_MODEL), F32),
        jax.ShapeDtypeStruct((batch, CONV_W - 1, CONV_CH), F32),
        jax.ShapeDtypeStruct((batch, DN_HEADS, HEAD_DIM, HEAD_DIM), F32),
    ]
    if emit_v:
        out_specs.append(pl.BlockSpec((bg, pv, SGU_WIDTH), lambda i, j: (i, j, 0)))
        out_shape.append(jax.ShapeDtypeStruct((batch, seq, SGU_WIDTH), F32))
    scratch = [
        pltpu.VMEM((bg, CONV_PAD, CONV_CH), F32),
        pltpu.VMEM((bg, CONV_PAD + p, CONV_CH), F32),
        pltpu.VMEM((rows, CONV_CH), F32),
        pltpu.VMEM((rows, AB_LANES), F32),
        pltpu.VMEM((rows, DN_WIDTH), F32),
        pltpu.VMEM((rows, D_MODEL), BF16),
    ]
    return pl.pallas_call(
        functools.partial(_mixer_kernel, bg=bg, p=p, pv=pv, emit_v=emit_v, zero_init=zero_init),
        grid=grid, in_specs=in_specs, out_specs=out_specs, out_shape=out_shape,
        scratch_shapes=scratch,
        compiler_params=pltpu.CompilerParams(
            dimension_semantics=("arbitrary", "arbitrary"), vmem_limit_bytes=VMEM_LIMIT_BYTES),
        name="mixer",
    )(*args)


def _ffn_kernel(x_ref, mod_ref, ng_ref, wup_ref, wdn_ref, fg_ref, o_ref, *scratch, bb, tm, ffc, final_norm):
    mod = mod_ref[...]

    def normed():
        x3 = x_ref[...]
        ms = jnp.mean(x3 * x3, axis=-1, keepdims=True)
        gain = ng_ref[...] * (1.0 + mod[:, 4:5, :])
        h3 = (x3 * lax.rsqrt(ms + EPS)) * gain + mod[:, 3:4, :]
        return h3.reshape(bb * tm, D_MODEL).astype(BF16)

    def hidden(hb):
        a = jnp.maximum(_dot(hb, wup_ref[...]), 0.0)
        return _dot((a * a).astype(BF16), wdn_ref[...])

    def finish(down):
        y = x_ref[...] + mod[:, 5:6, :] * down.reshape(bb, tm, D_MODEL)
        if final_norm:
            yms = jnp.mean(y * y, axis=-1, keepdims=True)
            y = (y * lax.rsqrt(yms + EPS)) * fg_ref[...]
        o_ref[...] = y

    if ffc == 1:
        finish(hidden(normed()))
        return
    hbuf, acc = scratch
    c = pl.program_id(2)

    @pl.when(c == 0)
    def _():
        hbuf[...] = normed()
        acc[...] = jnp.zeros(acc.shape, F32)

    acc[...] += hidden(hbuf[...])

    @pl.when(c == ffc - 1)
    def _():
        finish(acc[...])


def _ffn(x, mod4, prm, *, layer, mod_off, bb, tm, ffc, final_norm):
    batch, seq, _ = x.shape
    mod_blk = mod_off // bb
    scratch = [] if ffc == 1 else [pltpu.VMEM((bb * tm, D_MODEL), BF16), pltpu.VMEM((bb * tm, D_MODEL), F32)]
    return pl.pallas_call(
        functools.partial(_ffn_kernel, bb=bb, tm=tm, ffc=ffc, final_norm=final_norm),
        grid=(batch // bb, seq // tm, ffc),
        in_specs=[
            pl.BlockSpec((bb, tm, D_MODEL), lambda i, j, c: (i, j, 0)),
            pl.BlockSpec((None, bb, 6, D_MODEL), lambda i, j, c: (layer, i + mod_blk, 0, 0)),
            pl.BlockSpec((None, 1, D_MODEL), lambda i, j, c: (layer, 0, 0)),
            pl.BlockSpec((None, D_MODEL, D_FF // ffc), lambda i, j, c: (layer, 0, c)),
            pl.BlockSpec((None, D_FF // ffc, D_MODEL), lambda i, j, c: (layer, c, 0)),
            pl.BlockSpec((1, D_MODEL), lambda i, j, c: (0, 0)),
        ],
        out_specs=pl.BlockSpec((bb, tm, D_MODEL), lambda i, j, c: (i, j, 0)),
        out_shape=jax.ShapeDtypeStruct((batch, seq, D_MODEL), F32),
        scratch_shapes=scratch,
        compiler_params=pltpu.CompilerParams(
            dimension_semantics=("arbitrary", "arbitrary", "arbitrary"), vmem_limit_bytes=VMEM_LIMIT_BYTES),
        name="ffn",
    )(x, mod4, prm["ng_ffn"], prm["wup"], prm["wdn"], prm["fg"])


def _trunk(x, mod4, conv_in, delta_in, prm, sgw, sgb, *, mod_off, bg, p, pv, ffn_bb, ffn_tm, ffn_ffc, emit_v):
    convs, deltas, vrows = [], [], []
    for l in range(DEPTH):
        outs = _mixer(x, mod4, conv_in, delta_in, prm, sgw, sgb, layer=l, mod_off=mod_off,
                      bg=bg, p=p, pv=pv, emit_v=emit_v)
        convs.append(outs[1])
        deltas.append(outs[2])
        if emit_v:
            vrows.append(outs[3])
        x = _ffn(outs[0], mod4, prm, layer=l, mod_off=mod_off, bb=ffn_bb, tm=ffn_tm, ffc=ffn_ffc,
                 final_norm=(l == DEPTH - 1))
    return x, jnp.stack(convs), jnp.stack(deltas), (jnp.stack(vrows) if emit_v else None)


def kernel(x_prompt, x_sample, c_prompt, c_sample, state_conv, state_delta, ada_w, ada_b, norm_mix_g,
           norm_ffn_g, w_in, sgu_norm_g, sgu_w, sgu_b, conv_w, dt_bias, a_log, dn_norm_g, w_out, w_up,
           w_down, final_norm_g):
    nb, seq, _ = x_prompt.shape
    ns, dec_seq, _ = x_sample.shape
    sgu_chunk = sgu_w.shape[-1]

    head_lanes = ((0, 0), (G_LANE0, AB_LANES - 2 * DN_HEADS))
    prm = dict(
        ng_mix=norm_mix_g[:, None, :], ng_ffn=norm_ffn_g[:, None, :], sgng=sgu_norm_g[:, None, :],
        win=w_in[:, :, :O_AB].astype(BF16),
        wab=jnp.pad(w_in[:, :, O_AB:], ((0, 0), (0, 0), (0, AB_LANES - 2 * DN_HEADS))).astype(BF16),
        cw=conv_w, dtb=jnp.pad(dt_bias, head_lanes)[:, None, :],
        alog=jnp.pad(a_log, head_lanes)[:, None, :], dng=dn_norm_g[:, None, :],
        wout=w_out.astype(BF16), wup=w_up.astype(BF16), wdn=w_down.astype(BF16), fg=final_norm_g[None, :])
    sgb_p = jnp.broadcast_to(sgu_b[:, :, :, None], (DEPTH, SGU_GROUPS, sgu_chunk, GROUP_DIM))
    pad = CHUNK - dec_seq
    sgw_s = jnp.pad(sgu_w[:, :, :dec_seq, :dec_seq], ((0, 0), (0, 0), (0, pad), (0, pad)))
    sgb_s = jnp.pad(sgb_p[:, :, :dec_seq, :], ((0, 0), (0, 0), (0, pad), (0, 0)))
    conv_in = jnp.pad(state_conv, ((0, 0), (0, 0), (CONV_PAD - (CONV_W - 1), 0), (0, 0)))

    mod_all = _modulation(jnp.concatenate([c_prompt, c_sample], axis=0), ada_w, ada_b)
    mod4 = mod_all.reshape(DEPTH, nb + ns, 6, D_MODEL)

    y_prompt, prompt_conv, prompt_delta, _ = _trunk(
        x_prompt, mod4, None, None, prm, sgu_w, sgb_p,
        mod_off=0, bg=4, p=sgu_chunk, pv=sgu_chunk, ffn_bb=1, ffn_tm=512, ffn_ffc=1, emit_v=False)

    y_sample, sample_conv, sample_delta, sample_v = _trunk(
        x_sample, mod4, conv_in, state_delta, prm, sgw_s, sgb_s,
        mod_off=nb, bg=ns, p=CHUNK, pv=dec_seq, ffn_bb=ns, ffn_tm=dec_seq, ffn_ffc=4, emit_v=True)

    return (y_prompt, y_sample, prompt_conv, prompt_delta, sample_conv, sample_delta, sample_v)
```

```python
import functools

import jax
import jax.numpy as jnp
from jax import lax
from jax.experimental import pallas as pl
from jax.experimental.pallas import tpu as pltpu

F32 = jnp.float32
BF16 = jnp.bfloat16

D_MODEL = 1024
DEPTH = 2
CHUNK = 64
CHUNK_LOG2 = 6
SGU_WIDTH = 512
SGU_GROUPS = 4
GROUP_DIM = 128
DN_WIDTH = 512
DN_HEADS = 4
HEAD_DIM = 128
CONV_W = 4
CONV_CH = 3 * DN_WIDTH
D_FF = 4 * D_MODEL
EPS = 1e-6

O_UV = 0
O_QKV = 2 * SGU_WIDTH
O_Z = O_QKV + CONV_CH
O_AB = O_Z + DN_WIDTH
AB_LANES = 128
IN_WIDTH = O_AB + 2 * DN_HEADS
IN_PAD = O_AB + AB_LANES
G_LANE0 = DN_HEADS
CONV_PAD = 8

VMEM_LIMIT_BYTES = 56 * 1024 * 1024
MXU_COLS = 256


def _sigmoid(x):
    return 0.5 * (1.0 + jnp.tanh(0.5 * x))


def _silu(x):
    h = 0.5 * x
    return h + h * jnp.tanh(h)


def _softplus(x):
    return jnp.maximum(x, 0.0) + jnp.log(1.0 + jnp.exp(-jnp.abs(x)))


def _gelu_tanh(x):
    c = 0.7978845608028654
    h = 0.5 * x
    return h + h * jnp.tanh(x * (c + (c * 0.044715) * (x * x)))


def _dot(a, b):
    return jnp.dot(a, b, preferred_element_type=F32)


def _dot_nt(a, b):
    return lax.dot_general(a, b, (((1,), (1,)), ((), ())), preferred_element_type=F32)


def _split3(x):
    hi = x.astype(BF16)
    r = x - hi.astype(F32)
    mid = r.astype(BF16)
    lo = (r - mid.astype(F32)).astype(BF16)
    return hi, mid, lo


def _blockdiag_lanes128(y):
    yb = y.astype(BF16)
    z = jnp.zeros((y.shape[0], HEAD_DIM), BF16)
    top = jnp.concatenate([yb[:, :HEAD_DIM], z], axis=1)
    bot = jnp.concatenate([z, yb[:, HEAD_DIM:]], axis=1)
    return jnp.concatenate([top, bot], axis=0)


def _interleave(groups):
    keyed = [((i + 0.5) / len(g), gi, i, f) for gi, g in enumerate(groups) for i, f in enumerate(g)]
    return [f for _, _, _, f in sorted(keyed, key=lambda e: e[:3])]


def _layer_spec(shape, layer):
    return pl.BlockSpec((None,) + tuple(shape), lambda i, j: (layer,) + (0,) * len(shape))


MOD_BLOCK = 2048


def _mod_kernel(c_ref, w_ref, b_ref, o_ref):
    cs = _silu(c_ref[...]).astype(BF16)
    o_ref[...] = _dot(cs, w_ref[...].astype(BF16)) + b_ref[...]


def _modulation(c_all, ada_w, ada_b):
    n = c_all.shape[0]
    width = ada_w.shape[2]
    return pl.pallas_call(
        _mod_kernel,
        grid=(DEPTH, width // MOD_BLOCK),
        in_specs=[
            pl.BlockSpec((n, D_MODEL), lambda l, j: (0, 0)),
            pl.BlockSpec((None, D_MODEL, MOD_BLOCK), lambda l, j: (l, 0, j)),
            pl.BlockSpec((None, 1, MOD_BLOCK), lambda l, j: (l, 0, j)),
        ],
        out_specs=pl.BlockSpec((None, n, MOD_BLOCK), lambda l, j: (l, 0, j)),
        out_shape=jax.ShapeDtypeStruct((DEPTH, n, width), F32),
        compiler_params=pltpu.CompilerParams(
            dimension_semantics=("arbitrary", "arbitrary"), vmem_limit_bytes=VMEM_LIMIT_BYTES),
        name="adaln_modulation",
    )(c_all, ada_w, ada_b.reshape(DEPTH, 1, width))


W_IN_ROWS = 256


def _cast_pad_kernel(w_ref, o_ref):
    o_ref[:, 0:O_AB] = w_ref[:, 0:O_AB].astype(BF16)
    logits = w_ref[:, O_AB:IN_WIDTH]
    zeros = jnp.zeros((logits.shape[0], IN_PAD - IN_WIDTH), F32)
    o_ref[:, O_AB:IN_PAD] = jnp.concatenate([logits, zeros], axis=1).astype(BF16)


def _cast_pad_w_in(w_in):
    return pl.pallas_call(
        _cast_pad_kernel,
        grid=(DEPTH, D_MODEL // W_IN_ROWS),
        in_specs=[pl.BlockSpec((None, W_IN_ROWS, IN_WIDTH), lambda l, i: (l, i, 0))],
        out_specs=pl.BlockSpec((None, W_IN_ROWS, IN_PAD), lambda l, i: (l, i, 0)),
        out_shape=jax.ShapeDtypeStruct((DEPTH, D_MODEL, IN_PAD), BF16),
        compiler_params=pltpu.CompilerParams(
            dimension_semantics=("arbitrary", "arbitrary"), vmem_limit_bytes=VMEM_LIMIT_BYTES),
        name="cast_pad_w_in",
    )(w_in)


def _mixer_kernel(*refs, bg, p, pv, emit_v, zero_init):
    refs = list(refs)
    x_ref, mod_ref = refs[:2]
    refs = refs[2:]
    if not zero_init:
        conv0_ref, delta0_ref = refs[:2]
        refs = refs[2:]
    (ng_ref, win_ref, sgng_ref, sgw_ref, sgb_ref, cw_ref, dtb_ref, alog_ref, dng_ref, wout_ref,
     xo_ref, convo_ref, deltao_ref) = refs[:13]
    refs = refs[13:]
    if emit_v:
        vo_ref = refs[0]
        refs = refs[1:]
    ctail, cbuf, act, abl, zs, mixb = refs
    nchunk = p // CHUNK
    step = pl.program_id(1)

    @pl.when(step == 0)
    def _():
        if zero_init:
            ctail[...] = jnp.zeros(ctail.shape, F32)
            deltao_ref[...] = jnp.zeros(deltao_ref.shape, F32)
        else:
            ctail[...] = conv0_ref[...]
            deltao_ref[...] = delta0_ref[...]

    nhalf = 2 if bg % 2 == 0 else 1
    hbg = bg // nhalf
    hrows = hbg * p
    mod = mod_ref[...]
    cw = cw_ref[...]
    ri = lax.broadcasted_iota(jnp.int32, (p, p), 0)
    ci = lax.broadcasted_iota(jnp.int32, (p, p), 1)
    sg_mask = (ci >> CHUNK_LOG2) <= (ri >> CHUNK_LOG2)
    sg_w = [jnp.where(sg_mask, sgw_ref[g], 0.0).astype(BF16) for g in range(SGU_GROUPS)]
    state = [dict() for _ in range(nhalf)]
    pre = {}

    def stage_norm(h):
        st, b0 = state[h], h * hbg

        def run():
            x3 = x_ref[b0:b0 + hbg]
            ms = jnp.mean(x3 * x3, axis=-1, keepdims=True)
            gain = ng_ref[...] * (1.0 + mod[b0:b0 + hbg, 1:2, :])
            h3 = (x3 * lax.rsqrt(ms + EPS)) * gain + mod[b0:b0 + hbg, 0:1, :]
            if pv < p:
                h3 = jnp.concatenate([h3, jnp.zeros((hbg, p - pv, D_MODEL), F32)], axis=1)
            st["hb"] = h3.reshape(hrows, D_MODEL).astype(BF16)
        return [run]

    def stage_uv(h):
        st = state[h]
        st["uv"] = [None] * (O_QKV // MXU_COLS)

        def piece(n):
            def run():
                st["uv"][n] = _dot(st["hb"], win_ref[:, O_UV + n * MXU_COLS:O_UV + (n + 1) * MXU_COLS])
            return run
        return [piece(n) for n in range(O_QKV // MXU_COLS)]

    def stage_gate(h):
        st, b0 = state[h], h * hbg
        nblk = O_QKV // MXU_COLS

        def act_piece(n):
            def run():
                st["uv"][n] = _gelu_tanh(st["uv"][n])
            return run

        def vnorm():
            v = jnp.concatenate(st["uv"][nblk // 2:], axis=1)
            vms = jnp.mean(v * v, axis=-1, keepdims=True)
            vn = (v * lax.rsqrt(vms + EPS)) * sgng_ref[...]
            if emit_v:
                vo_ref[b0:b0 + hbg] = vn.reshape(hbg, p, SGU_WIDTH)[:, :pv, :]
            st["vnb"] = vn.astype(BF16)
            st["u"] = jnp.concatenate(st["uv"][:nblk // 2], axis=1)

        def gate_piece(g, b):
            def run():
                lo, hi = g * GROUP_DIM, (g + 1) * GROUP_DIM
                r0 = b * p
                s = _dot(sg_w[g], st["vnb"][r0:r0 + p, lo:hi]) + sgb_ref[g]
                mixb[(b0 + b) * p:(b0 + b + 1) * p, lo:hi] = (st["u"][r0:r0 + p, lo:hi] * s).astype(BF16)
            return run
        return ([act_piece(n) for n in range(nblk)] + [vnorm]
                + [gate_piece(g, b) for g in range(SGU_GROUPS) for b in range(hbg)])

    def stage_qkv(h):
        st, b0 = state[h], h * hbg

        def piece(n):
            def run():
                if n == 0:
                    cbuf[b0:b0 + hbg, 0:CONV_PAD, :] = ctail[b0:b0 + hbg]
                lo, hi = n * MXU_COLS, (n + 1) * MXU_COLS
                cbuf[b0:b0 + hbg, CONV_PAD:CONV_PAD + p, lo:hi] = _dot(
                    st["hb"], win_ref[:, O_QKV + lo:O_QKV + hi]).reshape(hbg, p, MXU_COLS)
            return run
        return [piece(n) for n in range(CONV_CH // MXU_COLS)]

    def stage_conv(h):
        b0 = h * hbg
        r0 = b0 * p

        def piece(n):
            def run():
                lo, hi = n * MXU_COLS, (n + 1) * MXU_COLS
                y = cbuf[b0:b0 + hbg, CONV_PAD:CONV_PAD + p, lo:hi] * cw[CONV_W - 1:CONV_W, lo:hi]
                for i in range(1, CONV_W):
                    y = y + (cbuf[b0:b0 + hbg, CONV_PAD - i:CONV_PAD - i + p, lo:hi]
                             * cw[CONV_W - 1 - i:CONV_W - i, lo:hi])
                a = _silu(y).reshape(hrows, MXU_COLS)
                for lo_h in range(lo, hi, HEAD_DIM):
                    t = a[:, lo_h - lo:lo_h - lo + HEAD_DIM]
                    if lo_h < 2 * DN_WIDTH:
                        scale = HEAD_DIM ** -0.5 if lo_h < DN_WIDTH else 1.0
                        t = t * (lax.rsqrt(jnp.sum(t * t, axis=-1, keepdims=True) + EPS) * scale)
                    act[r0:r0 + hrows, lo_h:lo_h + HEAD_DIM] = t
            return run

        def tail():
            ctail[b0:b0 + hbg] = cbuf[b0:b0 + hbg, pv:pv + CONV_PAD, :]
            convo_ref[b0:b0 + hbg] = cbuf[b0:b0 + hbg, pv + CONV_PAD - (CONV_W - 1):pv + CONV_PAD, :]
        return [piece(n) for n in range(CONV_CH // MXU_COLS)] + [tail]

    def stage_zab(h):
        st = state[h]
        r0 = h * hrows

        def z_piece(n):
            def run():
                lo, hi = n * MXU_COLS, (n + 1) * MXU_COLS
                zs[r0:r0 + hrows, lo:hi] = _silu(_dot(st["hb"], win_ref[:, O_Z + lo:O_Z + hi]))
            return run

        def ab_piece():
            abl[r0:r0 + hrows, :] = _dot(st["hb"], win_ref[:, O_AB:IN_PAD])
        return [z_piece(n) for n in range(DN_WIDTH // MXU_COLS)] + [ab_piece]

    ii = lax.broadcasted_iota(jnp.int32, (CHUNK, 2 * HEAD_DIM), 0)
    lane4 = lax.broadcasted_iota(jnp.int32, (CHUNK, 2 * HEAD_DIM), 1)
    jj = lane4 & (CHUNK - 1)
    incl = ii >= jj
    strict = ii > jj
    eye4 = jnp.where(ii == jj, 1.0, 0.0).astype(F32)
    head_of_lane = lane4 >> CHUNK_LOG2
    left2 = lax.broadcasted_iota(jnp.int32, (CHUNK, HEAD_DIM), 1) < CHUNK
    kk = lax.broadcasted_iota(jnp.int32, (CHUNK, 3 * CHUNK), 1) & (CHUNK - 1)
    rr = lax.broadcasted_iota(jnp.int32, (CHUNK, 3 * CHUNK), 0)
    ltri3 = jnp.where(kk <= rr, 1.0, 0.0).astype(BF16)
    level_masks = []
    for k in range(CHUNK_LOG2):
        level_masks.append(jnp.where(((ii ^ jj) >> k) == 1, (ii >> k) & 1, 0) == 1)
    dtb = dtb_ref[...]
    neg_decay = -jnp.exp(alog_ref[...])
    dng = dng_ref[...]
    pos = lax.broadcasted_iota(jnp.int32, (CHUNK, AB_LANES), 0)

    def lanes(t, lane0):
        return jnp.concatenate(
            [jnp.broadcast_to(t[:, lane0 + hh:lane0 + hh + 1], (CHUNK, HEAD_DIM)) for hh in range(DN_HEADS)], axis=1)

    def pack4(t):
        c0 = jnp.where(left2, t[:, 0:128], t[:, 128:256])
        c1 = jnp.where(left2, t[:, 256:384], t[:, 384:512])
        return jnp.concatenate([c0, c1], axis=1)

    head_sel = [jnp.where(head_of_lane == hh, 1.0, 0.0).astype(BF16) for hh in range(DN_HEADS)]
    level_sel = [[jnp.where(head_of_lane == hh, jnp.where(level_masks[k], 1.0, 0.0), 0.0).astype(BF16)
                  for hh in range(DN_HEADS)] for k in range(CHUNK_LOG2)]

    def blockdiag4(yb, sel):
        return jnp.concatenate([yb * sel[hh] for hh in range(DN_HEADS)], axis=0)

    def stage_pre(h):
        def piece(b, c):
            def run():
                r0 = b * p + c * CHUNK
                ab = abl[r0:r0 + CHUNK, :]
                beta_c = _sigmoid(ab)
                g_c = neg_decay * _softplus(ab + dtb)
                if pv < p:
                    valid = (pos + c * CHUNK) < pv
                    beta_c = jnp.where(valid, beta_c, 0.0)
                    g_c = jnp.where(valid, g_c, 0.0)
                gc_c = _dot(ltri3, jnp.concatenate(_split3(g_c), axis=0))
                glast_c = gc_c[CHUNK - 1:CHUNK, :]
                beta = lanes(beta_c, 0)
                egc = lanes(jnp.exp(gc_c), G_LANE0)
                ekg = lanes(jnp.exp(glast_c - gc_c), G_LANE0)
                gl = jnp.concatenate([jnp.broadcast_to(jnp.exp(glast_c)[:, G_LANE0 + hh:G_LANE0 + hh + 1],
                                                       (1, HEAD_DIM)) for hh in range(DN_HEADS)], axis=1)
                gc_t = gc_c.T
                grow4 = jnp.broadcast_to(
                    jnp.concatenate([gc_t[G_LANE0 + hh:G_LANE0 + hh + 1, :] for hh in range(DN_HEADS)], axis=1),
                    (CHUNK, DN_HEADS * CHUNK))
                diff4 = pack4(lanes(gc_c, G_LANE0)) - grow4
                dec_incl = jnp.exp(jnp.where(incl, diff4, -jnp.inf))
                dec_strict = jnp.where(strict, dec_incl, 0.0)
                qn = act[r0:r0 + CHUNK, 0:DN_WIDTH]
                kn = act[r0:r0 + CHUNK, DN_WIDTH:2 * DN_WIDTH]
                vv = act[r0:r0 + CHUNK, 2 * DN_WIDTH:3 * DN_WIDTH]
                qb = qn.astype(BF16)
                kb = kn.astype(BF16)
                qk_cols, kk_cols = [], []
                for pr in range(DN_HEADS // 2):
                    lo, hi = pr * 2 * HEAD_DIM, (pr + 1) * 2 * HEAD_DIM
                    lhs = jnp.concatenate([qb[:, lo:hi], kb[:, lo:hi]], axis=0)
                    r = _dot_nt(lhs, _blockdiag_lanes128(kn[:, lo:hi]))
                    qk_cols.append(r[:CHUNK])
                    kk_cols.append(r[CHUNK:])
                attn4 = jnp.concatenate(qk_cols, axis=1) * dec_incl
                a4 = jnp.concatenate(kk_cols, axis=1) * (pack4(beta) * dec_strict)
                pre[(b, c)] = dict(a4=a4, attn4=attn4.astype(BF16), beta=beta, vv=vv,
                                   kgc=(kn * egc).astype(BF16), qg=(qn * egc).astype(BF16),
                                   kg=kn * ekg, gl=gl)
            return run
        return [piece(b, c) for b in range(h * hbg, (h + 1) * hbg) for c in range(nchunk)]

    stages = [stage_norm, stage_uv, stage_gate, stage_qkv, stage_conv, stage_zab, stage_pre]
    for t in range(len(stages) + nhalf - 1):
        active = [stages[t - h](h) for h in range(nhalf) if 0 <= t - h < len(stages)]
        for run in _interleave(active):
            run()

    probs = [(b, c) for b in range(bg) for c in range(nchunk)]
    a4 = {key: pre[key]["a4"] for key in probs}
    a4b = {key: a4[key].astype(BF16) for key in probs}
    dinv = {key: eye4 - jnp.where(level_masks[0], a4[key], 0.0) for key in probs}
    for k in range(1, CHUNK_LOG2):
        db = {key: dinv[key].astype(BF16) for key in probs}
        p1 = {key: _dot(db[key], blockdiag4(a4b[key], level_sel[k])) for key in probs}
        p2 = {key: _dot(p1[key].astype(BF16), blockdiag4(db[key], head_sel)) for key in probs}
        dinv = {key: dinv[key] - p2[key] for key in probs}

    pairs = [(b, pr) for b in range(bg) for pr in range(DN_HEADS // 2)]
    for c in range(nchunk):
        s_pair, r, v_new, yy = {}, {}, {}, {}
        for (b, pr) in pairs:
            pc = pre[(b, c)]
            lo, hi = pr * 2 * HEAD_DIM, (pr + 1) * 2 * HEAD_DIM
            s_pair[b, pr] = jnp.concatenate([deltao_ref[b, 2 * pr], deltao_ref[b, 2 * pr + 1]], axis=1)
            lhs = jnp.concatenate([pc["kgc"][:, lo:hi], pc["qg"][:, lo:hi]], axis=0)
            r[b, pr] = _dot(lhs, _blockdiag_lanes128(s_pair[b, pr]))
        for (b, pr) in pairs:
            pc = pre[(b, c)]
            lo, hi = pr * 2 * HEAD_DIM, (pr + 1) * 2 * HEAD_DIM
            xr = pc["beta"][:, lo:hi] * (pc["vv"][:, lo:hi] - r[b, pr][:CHUNK])
            t_pair = dinv[(b, c)][:, pr * HEAD_DIM:(pr + 1) * HEAD_DIM].astype(BF16)
            v_new[b, pr] = _dot(t_pair, _blockdiag_lanes128(xr))
        for (b, pr) in pairs:
            pc = pre[(b, c)]
            lo, hi = pr * 2 * HEAD_DIM, (pr + 1) * 2 * HEAD_DIM
            kg_pair = jnp.concatenate([pc["kg"][:, lo:lo + HEAD_DIM], pc["kg"][:, lo + HEAD_DIM:hi]],
                                      axis=0)
            lhs2 = jnp.concatenate([pc["attn4"][:, pr * HEAD_DIM:(pr + 1) * HEAD_DIM],
                                    kg_pair.T.astype(BF16)], axis=0)
            yy[b, pr] = _dot(lhs2, _blockdiag_lanes128(v_new[b, pr]))
        for (b, pr) in pairs:
            lo, hi = pr * 2 * HEAD_DIM, (pr + 1) * 2 * HEAD_DIM
            s_new = s_pair[b, pr] * pre[(b, c)]["gl"][:, lo:hi] + yy[b, pr][CHUNK:]
            deltao_ref[b, 2 * pr] = s_new[:, :HEAD_DIM]
            deltao_ref[b, 2 * pr + 1] = s_new[:, HEAD_DIM:]
        for b in range(bg):
            r0 = b * p + c * CHUNK
            for hh in range(DN_HEADS):
                lo = (hh % 2) * HEAD_DIM
                o_h = r[b, hh // 2][CHUNK:, lo:lo + HEAD_DIM] + yy[b, hh // 2][:CHUNK, lo:lo + HEAD_DIM]
                oms = jnp.mean(o_h * o_h, axis=-1, keepdims=True)
                gated = (o_h * lax.rsqrt(oms + EPS)) * dng * zs[r0:r0 + CHUNK, hh * HEAD_DIM:(hh + 1) * HEAD_DIM]
                mixb[r0:r0 + CHUNK, SGU_WIDTH + hh * HEAD_DIM:SGU_WIDTH + (hh + 1) * HEAD_DIM] = gated.astype(BF16)

    out = _dot(mixb[...], wout_ref[...]).reshape(bg, p, D_MODEL)
    xo_ref[...] = x_ref[...] + mod[:, 2:3, :] * out[:, :pv, :]


def _mixer(x, mod4, conv_in, delta_in, prm, sgw, sgb, *, layer, mod_off, bg, p, pv, emit_v):
    batch, seq, _ = x.shape
    zero_init = conv_in is None
    grid = (batch // bg, seq // pv)
    rows = bg * p
    mod_blk = mod_off // bg
    in_specs = [
        pl.BlockSpec((bg, pv, D_MODEL), lambda i, j: (i, j, 0)),
        pl.BlockSpec((None, bg, 6, D_MODEL), lambda i, j: (layer, i + mod_blk, 0, 0)),
    ]
    args = [x, mod4]
    if not zero_init:
        in_specs += [
            pl.BlockSpec((None, bg, CONV_PAD, CONV_CH), lambda i, j: (layer, i, 0, 0)),
            pl.BlockSpec((None, bg, DN_HEADS, HEAD_DIM, HEAD_DIM), lambda i, j: (layer, i, 0, 0, 0)),
        ]
        args += [conv_in, delta_in]
    in_specs += [
        _layer_spec((1, D_MODEL), layer), _layer_spec((D_MODEL, IN_PAD), layer),
        _layer_spec((1, SGU_WIDTH), layer), _layer_spec((SGU_GROUPS, p, p), layer),
        _layer_spec((SGU_GROUPS, p, GROUP_DIM), layer), _layer_spec((CONV_W, CONV_CH), layer),
        _layer_spec((1, AB_LANES), layer), _layer_spec((1, AB_LANES), layer),
        _layer_spec((1, HEAD_DIM), layer), _layer_spec((D_MODEL, D_MODEL), layer),
    ]
    args += [prm["ng_mix"], prm["win"], prm["sgng"], sgw, sgb, prm["cw"], prm["dtb"], prm["alog"],
             prm["dng"], prm["wout"]]
    out_specs = [
        pl.BlockSpec((bg, pv, D_MODEL), lambda i, j: (i, j, 0)),
        pl.BlockSpec((bg, CONV_W - 1, CONV_CH), lambda i, j: (i, 0, 0)),
        pl.BlockSpec((bg, DN_HEADS, HEAD_DIM, HEAD_DIM), lambda i, j: (i, 0, 0, 0)),
    ]
    out_shape = [
        jax.ShapeDtypeStruct((batch, seq, D_MODEL), F32),
        jax.ShapeDtypeStruct((batch, CONV_W - 1, CONV_CH), F32),
        jax.ShapeDtypeStruct((batch, DN_HEADS, HEAD_DIM, HEAD_DIM), F32),
    ]
    if emit_v:
        out_specs.append(pl.BlockSpec((bg, pv, SGU_WIDTH), lambda i, j: (i, j, 0)))
        out_shape.append(jax.ShapeDtypeStruct((batch, seq, SGU_WIDTH), F32))
    scratch = [
        pltpu.VMEM((bg, CONV_PAD, CONV_CH), F32),
        pltpu.VMEM((bg, CONV_PAD + p, CONV_CH), F32),
        pltpu.VMEM((rows, CONV_CH), F32),
        pltpu.VMEM((rows, AB_LANES), F32),
        pltpu.VMEM((rows, DN_WIDTH), F32),
        pltpu.VMEM((rows, D_MODEL), BF16),
    ]
    return pl.pallas_call(
        functools.partial(_mixer_kernel, bg=bg, p=p, pv=pv, emit_v=emit_v, zero_init=zero_init),
        grid=grid, in_specs=in_specs, out_specs=out_specs, out_shape=out_shape,
        scratch_shapes=scratch,
        compiler_params=pltpu.CompilerParams(
            dimension_semantics=("arbitrary", "arbitrary"), vmem_limit_bytes=VMEM_LIMIT_BYTES),
        name="mixer",
    )(*args)


def _ffn_kernel(x_ref, mod_ref, ng_ref, wup_ref, wdn_ref, fg_ref, o_ref, *scratch, bb, tm, ffc, final_norm):
    mod = mod_ref[...]

    def normed():
        x3 = x_ref[...]
        ms = jnp.mean(x3 * x3, axis=-1, keepdims=True)
        gain = ng_ref[...] * (1.0 + mod[:, 4:5, :])
        h3 = (x3 * lax.rsqrt(ms + EPS)) * gain + mod[:, 3:4, :]
        return h3.reshape(bb * tm, D_MODEL).astype(BF16)

    def hidden(hb):
        a = jnp.maximum(_dot(hb, wup_ref[...]), 0.0)
        return _dot((a * a).astype(BF16), wdn_ref[...])

    def finish(down):
        y = x_ref[...] + mod[:, 5:6, :] * down.reshape(bb, tm, D_MODEL)
        if final_norm:
            yms = jnp.mean(y * y, axis=-1, keepdims=True)
            y = (y * lax.rsqrt(yms + EPS)) * fg_ref[...]
        o_ref[...] = y

    if ffc == 1:
        finish(hidden(normed()))
        return
    hbuf, acc = scratch
    c = pl.program_id(2)

    @pl.when(c == 0)
    def _():
        hbuf[...] = normed()
        acc[...] = jnp.zeros(acc.shape, F32)

    acc[...] += hidden(hbuf[...])

    @pl.when(c == ffc - 1)
    def _():
        finish(acc[...])


def _ffn(x, mod4, prm, *, layer, mod_off, bb, tm, ffc, final_norm):
    batch, seq, _ = x.shape
    mod_blk = mod_off // bb
    scratch = [] if ffc == 1 else [pltpu.VMEM((bb * tm, D_MODEL), BF16), pltpu.VMEM((bb * tm, D_MODEL), F32)]
    return pl.pallas_call(
        functools.partial(_ffn_kernel, bb=bb, tm=tm, ffc=ffc, final_norm=final_norm),
        grid=(batch // bb, seq // tm, ffc),
        in_specs=[
            pl.BlockSpec((bb, tm, D_MODEL), lambda i, j, c: (i, j, 0)),
            pl.BlockSpec((None, bb, 6, D_MODEL), lambda i, j, c: (layer, i + mod_blk, 0, 0)),
            pl.BlockSpec((None, 1, D_MODEL), lambda i, j, c: (layer, 0, 0)),
            pl.BlockSpec((None, D_MODEL, D_FF // ffc), lambda i, j, c: (layer, 0, c)),
            pl.BlockSpec((None, D_FF // ffc, D_MODEL), lambda i, j, c: (layer, c, 0)),
            pl.BlockSpec((1, D_MODEL), lambda i, j, c: (0, 0)),
        ],
        out_specs=pl.BlockSpec((bb, tm, D_MODEL), lambda i, j, c: (i, j, 0)),
        out_shape=jax.ShapeDtypeStruct((batch, seq, D_MODEL), F32),
        scratch_shapes=scratch,
        compiler_params=pltpu.CompilerParams(
            dimension_semantics=("arbitrary", "arbitrary", "arbitrary"), vmem_limit_bytes=VMEM_LIMIT_BYTES),
        name="ffn",
    )(x, mod4, prm["ng_ffn"], prm["wup"], prm["wdn"], prm["fg"])


def _trunk(x, mod4, conv_in, delta_in, prm, sgw, sgb, *, mod_off, bg, p, pv, ffn_bb, ffn_tm, ffn_ffc, emit_v):
    convs, deltas, vrows = [], [], []
    for l in range(DEPTH):
        outs = _mixer(x, mod4, conv_in, delta_in, prm, sgw, sgb, layer=l, mod_off=mod_off,
                      bg=bg, p=p, pv=pv, emit_v=emit_v)
        convs.append(outs[1])
        deltas.append(outs[2])
        if emit_v:
            vrows.append(outs[3])
        x = _ffn(outs[0], mod4, prm, layer=l, mod_off=mod_off, bb=ffn_bb, tm=ffn_tm, ffc=ffn_ffc,
                 final_norm=(l == DEPTH - 1))
    return x, jnp.stack(convs), jnp.stack(deltas), (jnp.stack(vrows) if emit_v else None)


def kernel(x_prompt, x_sample, c_prompt, c_sample, state_conv, state_delta, ada_w, ada_b, norm_mix_g,
           norm_ffn_g, w_in, sgu_norm_g, sgu_w, sgu_b, conv_w, dt_bias, a_log, dn_norm_g, w_out, w_up,
           w_down, final_norm_g):
    nb, seq, _ = x_prompt.shape
    ns, dec_seq, _ = x_sample.shape
    sgu_chunk = sgu_w.shape[-1]

    head_lanes = ((0, 0), (G_LANE0, AB_LANES - 2 * DN_HEADS))
    prm = dict(
        ng_mix=norm_mix_g[:, None, :], ng_ffn=norm_ffn_g[:, None, :], sgng=sgu_norm_g[:, None, :],
        win=_cast_pad_w_in(w_in),
        cw=conv_w, dtb=jnp.pad(dt_bias, head_lanes)[:, None, :],
        alog=jnp.pad(a_log, head_lanes)[:, None, :], dng=dn_norm_g[:, None, :],
        wout=w_out.astype(BF16), wup=w_up.astype(BF16), wdn=w_down.astype(BF16), fg=final_norm_g[None, :])
    sgb_p = jnp.broadcast_to(sgu_b[:, :, :, None], (DEPTH, SGU_GROUPS, sgu_chunk, GROUP_DIM))
    pad = CHUNK - dec_seq
    sgw_s = jnp.pad(sgu_w[:, :, :dec_seq, :dec_seq], ((0, 0), (0, 0), (0, pad), (0, pad)))
    sgb_s = jnp.pad(sgb_p[:, :, :dec_seq, :], ((0, 0), (0, 0), (0, pad), (0, 0)))
    conv_in = jnp.pad(state_conv, ((0, 0), (0, 0), (CONV_PAD - (CONV_W - 1), 0), (0, 0)))

    mod_all = _modulation(jnp.concatenate([c_prompt, c_sample], axis=0), ada_w, ada_b)
    mod4 = mod_all.reshape(DEPTH, nb + ns, 6, D_MODEL)

    y_prompt, prompt_conv, prompt_delta, _ = _trunk(
        x_prompt, mod4, None, None, prm, sgu_w, sgb_p,
        mod_off=0, bg=4, p=sgu_chunk, pv=sgu_chunk, ffn_bb=1, ffn_tm=512, ffn_ffc=1, emit_v=False)

    y_sample, sample_conv, sample_delta, sample_v = _trunk(
        x_sample, mod4, conv_in, state_delta, prm, sgw_s, sgb_s,
        mod_off=nb, bg=ns, p=CHUNK, pv=dec_seq, ffn_bb=ns, ffn_tm=dec_seq, ffn_ffc=4, emit_v=True)

    return (y_prompt, y_sample, prompt_conv, prompt_delta, sample_conv, sample_delta, sample_v)
```

```python
import functools

import jax
import jax.numpy as jnp
from jax import lax
from jax.experimental import pallas as pl
from jax.experimental.pallas import tpu as pltpu

F32 = jnp.float32
BF16 = jnp.bfloat16

D_MODEL = 1024
DEPTH = 2
CHUNK = 64
CHUNK_LOG2 = 6
SGU_WIDTH = 512
SGU_GROUPS = 4
GROUP_DIM = 128
DN_WIDTH = 512
DN_HEADS = 4
HEAD_DIM = 128
CONV_W = 4
CONV_CH = 3 * DN_WIDTH
D_FF = 4 * D_MODEL
EPS = 1e-6

O_UV = 0
O_QKV = 2 * SGU_WIDTH
O_Z = O_QKV + CONV_CH
O_AB = O_Z + DN_WIDTH
AB_LANES = 128
IN_WIDTH = O_AB + 2 * DN_HEADS
IN_PAD = O_AB + AB_LANES
G_LANE0 = DN_HEADS
CONV_PAD = 8

VMEM_LIMIT_BYTES = 56 * 1024 * 1024
MXU_COLS = 256


def _sigmoid(x):
    return 0.5 * (1.0 + jnp.tanh(0.5 * x))


def _silu(x):
    h = 0.5 * x
    return h + h * jnp.tanh(h)


def _softplus(x):
    return jnp.maximum(x, 0.0) + jnp.log(1.0 + jnp.exp(-jnp.abs(x)))


def _gelu_tanh(x):
    c = 0.7978845608028654
    h = 0.5 * x
    return h + h * jnp.tanh(x * (c + (c * 0.044715) * (x * x)))


def _dot(a, b):
    return jnp.dot(a, b, preferred_element_type=F32)


def _dot_nt(a, b):
    return lax.dot_general(a, b, (((1,), (1,)), ((), ())), preferred_element_type=F32)


def _split3(x):
    hi = x.astype(BF16)
    r = x - hi.astype(F32)
    mid = r.astype(BF16)
    lo = (r - mid.astype(F32)).astype(BF16)
    return hi, mid, lo


def _blockdiag_lanes128(y):
    yb = y.astype(BF16)
    z = jnp.zeros((y.shape[0], HEAD_DIM), BF16)
    top = jnp.concatenate([yb[:, :HEAD_DIM], z], axis=1)
    bot = jnp.concatenate([z, yb[:, HEAD_DIM:]], axis=1)
    return jnp.concatenate([top, bot], axis=0)


def _interleave(groups):
    keyed = [((i + 0.5) / len(g), gi, i, f) for gi, g in enumerate(groups) for i, f in enumerate(g)]
    return [f for _, _, _, f in sorted(keyed, key=lambda e: e[:3])]


def _layer_spec(shape, layer):
    return pl.BlockSpec((None,) + tuple(shape), lambda i, j: (layer,) + (0,) * len(shape))


MOD_BLOCK = 2048


def _mod_kernel(c_ref, w_ref, b_ref, o_ref):
    cs = _silu(c_ref[...]).astype(BF16)
    o_ref[...] = _dot(cs, w_ref[...].astype(BF16)) + b_ref[...]


def _modulation(c_all, ada_w, ada_b):
    n = c_all.shape[0]
    width = ada_w.shape[2]
    return pl.pallas_call(
        _mod_kernel,
        grid=(DEPTH, width // MOD_BLOCK),
        in_specs=[
            pl.BlockSpec((n, D_MODEL), lambda l, j: (0, 0)),
            pl.BlockSpec((None, D_MODEL, MOD_BLOCK), lambda l, j: (l, 0, j)),
            pl.BlockSpec((None, 1, MOD_BLOCK), lambda l, j: (l, 0, j)),
        ],
        out_specs=pl.BlockSpec((None, n, MOD_BLOCK), lambda l, j: (l, 0, j)),
        out_shape=jax.ShapeDtypeStruct((DEPTH, n, width), F32),
        compiler_params=pltpu.CompilerParams(
            dimension_semantics=("arbitrary", "arbitrary"), vmem_limit_bytes=VMEM_LIMIT_BYTES),
        name="adaln_modulation",
    )(c_all, ada_w, ada_b.reshape(DEPTH, 1, width))


def _mixer_kernel(*refs, bg, p, pv, emit_v, zero_init):
    refs = list(refs)
    x_ref, mod_ref = refs[:2]
    refs = refs[2:]
    if not zero_init:
        conv0_ref, delta0_ref = refs[:2]
        refs = refs[2:]
    (ng_ref, win_ref, sgng_ref, sgw_ref, sgb_ref, cw_ref, dtb_ref, alog_ref, dng_ref, wout_ref,
     xo_ref, convo_ref, deltao_ref) = refs[:13]
    refs = refs[13:]
    if emit_v:
        vo_ref = refs[0]
        refs = refs[1:]
    ctail, cbuf, act, abl, zs, mixb = refs
    nchunk = p // CHUNK
    step = pl.program_id(1)

    @pl.when(step == 0)
    def _():
        if zero_init:
            ctail[...] = jnp.zeros(ctail.shape, F32)
            deltao_ref[...] = jnp.zeros(deltao_ref.shape, F32)
        else:
            ctail[...] = conv0_ref[...]
            deltao_ref[...] = delta0_ref[...]

    nhalf = 2 if bg % 2 == 0 else 1
    hbg = bg // nhalf
    hrows = hbg * p
    mod = mod_ref[...]
    cw = cw_ref[...]
    ri = lax.broadcasted_iota(jnp.int32, (p, p), 0)
    ci = lax.broadcasted_iota(jnp.int32, (p, p), 1)
    sg_mask = (ci >> CHUNK_LOG2) <= (ri >> CHUNK_LOG2)
    sg_w = [jnp.where(sg_mask, sgw_ref[g], 0.0).astype(BF16) for g in range(SGU_GROUPS)]
    state = [dict() for _ in range(nhalf)]
    pre = {}

    def stage_norm(h):
        st, b0 = state[h], h * hbg

        def run():
            x3 = x_ref[b0:b0 + hbg]
            ms = jnp.mean(x3 * x3, axis=-1, keepdims=True)
            gain = ng_ref[...] * (1.0 + mod[b0:b0 + hbg, 1:2, :])
            h3 = (x3 * lax.rsqrt(ms + EPS)) * gain + mod[b0:b0 + hbg, 0:1, :]
            if pv < p:
                h3 = jnp.concatenate([h3, jnp.zeros((hbg, p - pv, D_MODEL), F32)], axis=1)
            st["hb"] = h3.reshape(hrows, D_MODEL).astype(BF16)
        return [run]

    def stage_uv(h):
        st = state[h]
        st["uv"] = [None] * (O_QKV // MXU_COLS)

        def piece(n):
            def run():
                st["uv"][n] = _dot(st["hb"], win_ref[:, O_UV + n * MXU_COLS:O_UV + (n + 1) * MXU_COLS])
            return run
        return [piece(n) for n in range(O_QKV // MXU_COLS)]

    def stage_gate(h):
        st, b0 = state[h], h * hbg
        nblk = O_QKV // MXU_COLS

        def act_piece(n):
            def run():
                st["uv"][n] = _gelu_tanh(st["uv"][n])
            return run

        def vnorm():
            v = jnp.concatenate(st["uv"][nblk // 2:], axis=1)
            vms = jnp.mean(v * v, axis=-1, keepdims=True)
            vn = (v * lax.rsqrt(vms + EPS)) * sgng_ref[...]
            if emit_v:
                vo_ref[b0:b0 + hbg] = vn.reshape(hbg, p, SGU_WIDTH)[:, :pv, :]
            st["vnb"] = vn.astype(BF16)
            st["u"] = jnp.concatenate(st["uv"][:nblk // 2], axis=1)

        def gate_piece(g):
            def run():
                lo, hi = g * GROUP_DIM, (g + 1) * GROUP_DIM
                v_all = jnp.concatenate([st["vnb"][b * p:(b + 1) * p, lo:hi] for b in range(hbg)], axis=1)
                s_all = _dot(sg_w[g], v_all)
                for b in range(hbg):
                    s = s_all[:, b * GROUP_DIM:(b + 1) * GROUP_DIM] + sgb_ref[g]
                    mixb[(b0 + b) * p:(b0 + b + 1) * p, lo:hi] = (
                        st["u"][b * p:(b + 1) * p, lo:hi] * s).astype(BF16)
            return run
        return [act_piece(n) for n in range(nblk)] + [vnorm] + [gate_piece(g) for g in range(SGU_GROUPS)]

    def stage_qkv(h):
        st, b0 = state[h], h * hbg

        def piece(n):
            def run():
                if n == 0:
                    cbuf[b0:b0 + hbg, 0:CONV_PAD, :] = ctail[b0:b0 + hbg]
                lo, hi = n * MXU_COLS, (n + 1) * MXU_COLS
                cbuf[b0:b0 + hbg, CONV_PAD:CONV_PAD + p, lo:hi] = _dot(
                    st["hb"], win_ref[:, O_QKV + lo:O_QKV + hi]).reshape(hbg, p, MXU_COLS)
            return run
        return [piece(n) for n in range(CONV_CH // MXU_COLS)]

    def stage_conv(h):
        b0 = h * hbg
        r0 = b0 * p

        def piece(n):
            def run():
                lo, hi = n * MXU_COLS, (n + 1) * MXU_COLS
                y = cbuf[b0:b0 + hbg, CONV_PAD:CONV_PAD + p, lo:hi] * cw[CONV_W - 1:CONV_W, lo:hi]
                for i in range(1, CONV_W):
                    y = y + (cbuf[b0:b0 + hbg, CONV_PAD - i:CONV_PAD - i + p, lo:hi]
                             * cw[CONV_W - 1 - i:CONV_W - i, lo:hi])
                a = _silu(y).reshape(hrows, MXU_COLS)
                for lo_h in range(lo, hi, HEAD_DIM):
                    t = a[:, lo_h - lo:lo_h - lo + HEAD_DIM]
                    if lo_h < 2 * DN_WIDTH:
                        scale = HEAD_DIM ** -0.5 if lo_h < DN_WIDTH else 1.0
                        t = t * (lax.rsqrt(jnp.sum(t * t, axis=-1, keepdims=True) + EPS) * scale)
                    act[r0:r0 + hrows, lo_h:lo_h + HEAD_DIM] = t
            return run

        def tail():
            ctail[b0:b0 + hbg] = cbuf[b0:b0 + hbg, pv:pv + CONV_PAD, :]
            convo_ref[b0:b0 + hbg] = cbuf[b0:b0 + hbg, pv + CONV_PAD - (CONV_W - 1):pv + CONV_PAD, :]
        return [piece(n) for n in range(CONV_CH // MXU_COLS)] + [tail]

    def stage_zab(h):
        st = state[h]
        r0 = h * hrows

        def z_piece(n):
            def run():
                lo, hi = n * MXU_COLS, (n + 1) * MXU_COLS
                zs[r0:r0 + hrows, lo:hi] = _silu(_dot(st["hb"], win_ref[:, O_Z + lo:O_Z + hi]))
            return run

        def ab_piece():
            abl[r0:r0 + hrows, :] = _dot(st["hb"], win_ref[:, O_AB:IN_PAD])
        return [z_piece(n) for n in range(DN_WIDTH // MXU_COLS)] + [ab_piece]

    ii = lax.broadcasted_iota(jnp.int32, (CHUNK, 2 * HEAD_DIM), 0)
    lane4 = lax.broadcasted_iota(jnp.int32, (CHUNK, 2 * HEAD_DIM), 1)
    jj = lane4 & (CHUNK - 1)
    incl = ii >= jj
    strict = ii > jj
    eye4 = jnp.where(ii == jj, 1.0, 0.0).astype(F32)
    head_of_lane = lane4 >> CHUNK_LOG2
    left2 = lax.broadcasted_iota(jnp.int32, (CHUNK, HEAD_DIM), 1) < CHUNK
    kk = lax.broadcasted_iota(jnp.int32, (CHUNK, 3 * CHUNK), 1) & (CHUNK - 1)
    rr = lax.broadcasted_iota(jnp.int32, (CHUNK, 3 * CHUNK), 0)
    ltri3 = jnp.where(kk <= rr, 1.0, 0.0).astype(BF16)
    level_masks = []
    for k in range(CHUNK_LOG2):
        level_masks.append(jnp.where(((ii ^ jj) >> k) == 1, (ii >> k) & 1, 0) == 1)
    dtb = dtb_ref[...]
    neg_decay = -jnp.exp(alog_ref[...])
    dng = dng_ref[...]
    pos = lax.broadcasted_iota(jnp.int32, (CHUNK, AB_LANES), 0)

    def lanes(t, lane0):
        return jnp.concatenate(
            [jnp.broadcast_to(t[:, lane0 + hh:lane0 + hh + 1], (CHUNK, HEAD_DIM)) for hh in range(DN_HEADS)], axis=1)

    def pack4(t):
        c0 = jnp.where(left2, t[:, 0:128], t[:, 128:256])
        c1 = jnp.where(left2, t[:, 256:384], t[:, 384:512])
        return jnp.concatenate([c0, c1], axis=1)

    head_sel = [jnp.where(head_of_lane == hh, 1.0, 0.0).astype(BF16) for hh in range(DN_HEADS)]
    level_sel = [[jnp.where(head_of_lane == hh, jnp.where(level_masks[k], 1.0, 0.0), 0.0).astype(BF16)
                  for hh in range(DN_HEADS)] for k in range(CHUNK_LOG2)]

    def blockdiag4(yb, sel):
        return jnp.concatenate([yb * sel[hh] for hh in range(DN_HEADS)], axis=0)

    def stage_pre(h):
        def piece(b, c):
            def run():
                r0 = b * p + c * CHUNK
                ab = abl[r0:r0 + CHUNK, :]
                beta_c = _sigmoid(ab)
                g_c = neg_decay * _softplus(ab + dtb)
                if pv < p:
                    valid = (pos + c * CHUNK) < pv
                    beta_c = jnp.where(valid, beta_c, 0.0)
                    g_c = jnp.where(valid, g_c, 0.0)
                gc_c = _dot(ltri3, jnp.concatenate(_split3(g_c), axis=0))
                glast_c = gc_c[CHUNK - 1:CHUNK, :]
                beta = lanes(beta_c, 0)
                egc = lanes(jnp.exp(gc_c), G_LANE0)
                ekg = lanes(jnp.exp(glast_c - gc_c), G_LANE0)
                gl = jnp.concatenate([jnp.broadcast_to(jnp.exp(glast_c)[:, G_LANE0 + hh:G_LANE0 + hh + 1],
                                                       (1, HEAD_DIM)) for hh in range(DN_HEADS)], axis=1)
                gc_t = gc_c.T
                grow4 = jnp.broadcast_to(
                    jnp.concatenate([gc_t[G_LANE0 + hh:G_LANE0 + hh + 1, :] for hh in range(DN_HEADS)], axis=1),
                    (CHUNK, DN_HEADS * CHUNK))
                diff4 = pack4(lanes(gc_c, G_LANE0)) - grow4
                dec_incl = jnp.exp(jnp.where(incl, diff4, -jnp.inf))
                dec_strict = jnp.where(strict, dec_incl, 0.0)
                qn = act[r0:r0 + CHUNK, 0:DN_WIDTH]
                kn = act[r0:r0 + CHUNK, DN_WIDTH:2 * DN_WIDTH]
                vv = act[r0:r0 + CHUNK, 2 * DN_WIDTH:3 * DN_WIDTH]
                qb = qn.astype(BF16)
                kb = kn.astype(BF16)
                qk_cols, kk_cols = [], []
                for pr in range(DN_HEADS // 2):
                    lo, hi = pr * 2 * HEAD_DIM, (pr + 1) * 2 * HEAD_DIM
                    lhs = jnp.concatenate([qb[:, lo:hi], kb[:, lo:hi]], axis=0)
                    r = _dot_nt(lhs, _blockdiag_lanes128(kn[:, lo:hi]))
                    qk_cols.append(r[:CHUNK])
                    kk_cols.append(r[CHUNK:])
                attn4 = jnp.concatenate(qk_cols, axis=1) * dec_incl
                a4 = jnp.concatenate(kk_cols, axis=1) * (pack4(beta) * dec_strict)
                pre[(b, c)] = dict(a4=a4, attn4=attn4.astype(BF16), beta=beta, vv=vv,
                                   kgc=(kn * egc).astype(BF16), qg=(qn * egc).astype(BF16),
                                   kg=kn * ekg, gl=gl)
            return run
        return [piece(b, c) for b in range(h * hbg, (h + 1) * hbg) for c in range(nchunk)]

    stages = [stage_norm, stage_uv, stage_gate, stage_qkv, stage_conv, stage_zab, stage_pre]
    for t in range(len(stages) + nhalf - 1):
        active = [stages[t - h](h) for h in range(nhalf) if 0 <= t - h < len(stages)]
        for run in _interleave(active):
            run()

    probs = [(b, c) for b in range(bg) for c in range(nchunk)]
    a4 = {key: pre[key]["a4"] for key in probs}
    a4b = {key: a4[key].astype(BF16) for key in probs}
    dinv = {key: eye4 - jnp.where(level_masks[0], a4[key], 0.0) for key in probs}
    for k in range(1, CHUNK_LOG2):
        db = {key: dinv[key].astype(BF16) for key in probs}
        p1 = {key: _dot(db[key], blockdiag4(a4b[key], level_sel[k])) for key in probs}
        p2 = {key: _dot(p1[key].astype(BF16), blockdiag4(db[key], head_sel)) for key in probs}
        dinv = {key: dinv[key] - p2[key] for key in probs}

    pairs = [(b, pr) for b in range(bg) for pr in range(DN_HEADS // 2)]
    for c in range(nchunk):
        s_pair, r, v_new, yy = {}, {}, {}, {}
        for (b, pr) in pairs:
            pc = pre[(b, c)]
            lo, hi = pr * 2 * HEAD_DIM, (pr + 1) * 2 * HEAD_DIM
            s_pair[b, pr] = jnp.concatenate([deltao_ref[b, 2 * pr], deltao_ref[b, 2 * pr + 1]], axis=1)
            lhs = jnp.concatenate([pc["kgc"][:, lo:hi], pc["qg"][:, lo:hi]], axis=0)
            r[b, pr] = _dot(lhs, _blockdiag_lanes128(s_pair[b, pr]))
        for (b, pr) in pairs:
            pc = pre[(b, c)]
            lo, hi = pr * 2 * HEAD_DIM, (pr + 1) * 2 * HEAD_DIM
            xr = pc["beta"][:, lo:hi] * (pc["vv"][:, lo:hi] - r[b, pr][:CHUNK])
            t_pair = dinv[(b, c)][:, pr * HEAD_DIM:(pr + 1) * HEAD_DIM].astype(BF16)
            v_new[b, pr] = _dot(t_pair, _blockdiag_lanes128(xr))
        for (b, pr) in pairs:
            pc = pre[(b, c)]
            lo, hi = pr * 2 * HEAD_DIM, (pr + 1) * 2 * HEAD_DIM
            kg_pair = jnp.concatenate([pc["kg"][:, lo:lo + HEAD_DIM], pc["kg"][:, lo + HEAD_DIM:hi]],
                                      axis=0)
            lhs2 = jnp.concatenate([pc["attn4"][:, pr * HEAD_DIM:(pr + 1) * HEAD_DIM],
                                    kg_pair.T.astype(BF16)], axis=0)
            yy[b, pr] = _dot(lhs2, _blockdiag_lanes128(v_new[b, pr]))
        for (b, pr) in pairs:
            lo, hi = pr * 2 * HEAD_DIM, (pr + 1) * 2 * HEAD_DIM
            s_new = s_pair[b, pr] * pre[(b, c)]["gl"][:, lo:hi] + yy[b, pr][CHUNK:]
            deltao_ref[b, 2 * pr] = s_new[:, :HEAD_DIM]
            deltao_ref[b, 2 * pr + 1] = s_new[:, HEAD_DIM:]
        for b in range(bg):
            r0 = b * p + c * CHUNK
            for hh in range(DN_HEADS):
                lo = (hh % 2) * HEAD_DIM
                o_h = r[b, hh // 2][CHUNK:, lo:lo + HEAD_DIM] + yy[b, hh // 2][:CHUNK, lo:lo + HEAD_DIM]
                oms = jnp.mean(o_h * o_h, axis=-1, keepdims=True)
                gated = (o_h * lax.rsqrt(oms + EPS)) * dng * zs[r0:r0 + CHUNK, hh * HEAD_DIM:(hh + 1) * HEAD_DIM]
                mixb[r0:r0 + CHUNK, SGU_WIDTH + hh * HEAD_DIM:SGU_WIDTH + (hh + 1) * HEAD_DIM] = gated.astype(BF16)

    out = _dot(mixb[...], wout_ref[...]).reshape(bg, p, D_MODEL)
    xo_ref[...] = x_ref[...] + mod[:, 2:3, :] * out[:, :pv, :]


def _mixer(x, mod4, conv_in, delta_in, prm, sgw, sgb, *, layer, mod_off, bg, p, pv, emit_v):
    batch, seq, _ = x.shape
    zero_init = conv_in is None
    grid = (batch // bg, seq // pv)
    rows = bg * p
    mod_blk = mod_off // bg
    in_specs = [
        pl.BlockSpec((bg, pv, D_MODEL), lambda i, j: (i, j, 0)),
        pl.BlockSpec((None, bg, 6, D_MODEL), lambda i, j: (layer, i + mod_blk, 0, 0)),
    ]
    args = [x, mod4]
    if not zero_init:
        in_specs += [
            pl.BlockSpec((None, bg, CONV_PAD, CONV_CH), lambda i, j: (layer, i, 0, 0)),
            pl.BlockSpec((None, bg, DN_HEADS, HEAD_DIM, HEAD_DIM), lambda i, j: (layer, i, 0, 0, 0)),
        ]
        args += [conv_in, delta_in]
    in_specs += [
        _layer_spec((1, D_MODEL), layer), _layer_spec((D_MODEL, IN_PAD), layer),
        _layer_spec((1, SGU_WIDTH), layer), _layer_spec((SGU_GROUPS, p, p), layer),
        _layer_spec((SGU_GROUPS, p, GROUP_DIM), layer), _layer_spec((CONV_W, CONV_CH), layer),
        _layer_spec((1, AB_LANES), layer), _layer_spec((1, AB_LANES), layer),
        _layer_spec((1, HEAD_DIM), layer), _layer_spec((D_MODEL, D_MODEL), layer),
    ]
    args += [prm["ng_mix"], prm["win"], prm["sgng"], sgw, sgb, prm["cw"], prm["dtb"], prm["alog"],
             prm["dng"], prm["wout"]]
    out_specs = [
        pl.BlockSpec((bg, pv, D_MODEL), lambda i, j: (i, j, 0)),
        pl.BlockSpec((bg, CONV_W - 1, CONV_CH), lambda i, j: (i, 0, 0)),
        pl.BlockSpec((bg, DN_HEADS, HEAD_DIM, HEAD_DIM), lambda i, j: (i, 0, 0, 0)),
    ]
    out_shape = [
        jax.ShapeDtypeStruct((batch, seq, D_MODEL), F32),
        jax.ShapeDtypeStruct((batch, CONV_W - 1, CONV_CH), F32),
        jax.ShapeDtypeStruct((batch, DN_HEADS, HEAD_DIM, HEAD_DIM), F32),
    ]
    if emit_v:
        out_specs.append(pl.BlockSpec((bg, pv, SGU_WIDTH), lambda i, j: (i, j, 0)))
        out_shape.append(jax.ShapeDtypeStruct((batch, seq, SGU_WIDTH), F32))
    scratch = [
        pltpu.VMEM((bg, CONV_PAD, CONV_CH), F32),
        pltpu.VMEM((bg, CONV_PAD + p, CONV_CH), F32),
        pltpu.VMEM((rows, CONV_CH), F32),
        pltpu.VMEM((rows, AB_LANES), F32),
        pltpu.VMEM((rows, DN_WIDTH), F32),
        pltpu.VMEM((rows, D_MODEL), BF16),
    ]
    return pl.pallas_call(
        functools.partial(_mixer_kernel, bg=bg, p=p, pv=pv, emit_v=emit_v, zero_init=zero_init),
        grid=grid, in_specs=in_specs, out_specs=out_specs, out_shape=out_shape,
        scratch_shapes=scratch,
        compiler_params=pltpu.CompilerParams(
            dimension_semantics=("arbitrary", "arbitrary"), vmem_limit_bytes=VMEM_LIMIT_BYTES),
        name="mixer",
    )(*args)


def _ffn_kernel(x_ref, mod_ref, ng_ref, wup_ref, wdn_ref, fg_ref, o_ref, *scratch, bb, tm, ffc, final_norm):
    mod = mod_ref[...]

    def normed():
        x3 = x_ref[...]
        ms = jnp.mean(x3 * x3, axis=-1, keepdims=True)
        gain = ng_ref[...] * (1.0 + mod[:, 4:5, :])
        h3 = (x3 * lax.rsqrt(ms + EPS)) * gain + mod[:, 3:4, :]
        return h3.reshape(bb * tm, D_MODEL).astype(BF16)

    def hidden(hb):
        a = jnp.maximum(_dot(hb, wup_ref[...]), 0.0)
        return _dot((a * a).astype(BF16), wdn_ref[...])

    def finish(down):
        y = x_ref[...] + mod[:, 5:6, :] * down.reshape(bb, tm, D_MODEL)
        if final_norm:
            yms = jnp.mean(y * y, axis=-1, keepdims=True)
            y = (y * lax.rsqrt(yms + EPS)) * fg_ref[...]
        o_ref[...] = y

    if ffc == 1:
        finish(hidden(normed()))
        return
    hbuf, acc = scratch
    c = pl.program_id(2)

    @pl.when(c == 0)
    def _():
        hbuf[...] = normed()
        acc[...] = jnp.zeros(acc.shape, F32)

    acc[...] += hidden(hbuf[...])

    @pl.when(c == ffc - 1)
    def _():
        finish(acc[...])


def _ffn(x, mod4, prm, *, layer, mod_off, bb, tm, ffc, final_norm):
    batch, seq, _ = x.shape
    mod_blk = mod_off // bb
    scratch = [] if ffc == 1 else [pltpu.VMEM((bb * tm, D_MODEL), BF16), pltpu.VMEM((bb * tm, D_MODEL), F32)]
    return pl.pallas_call(
        functools.partial(_ffn_kernel, bb=bb, tm=tm, ffc=ffc, final_norm=final_norm),
        grid=(batch // bb, seq // tm, ffc),
        in_specs=[
            pl.BlockSpec((bb, tm, D_MODEL), lambda i, j, c: (i, j, 0)),
            pl.BlockSpec((None, bb, 6, D_MODEL), lambda i, j, c: (layer, i + mod_blk, 0, 0)),
            pl.BlockSpec((None, 1, D_MODEL), lambda i, j, c: (layer, 0, 0)),
            pl.BlockSpec((None, D_MODEL, D_FF // ffc), lambda i, j, c: (layer, 0, c)),
            pl.BlockSpec((None, D_FF // ffc, D_MODEL), lambda i, j, c: (layer, c, 0)),
            pl.BlockSpec((1, D_MODEL), lambda i, j, c: (0, 0)),
        ],
        out_specs=pl.BlockSpec((bb, tm, D_MODEL), lambda i, j, c: (i, j, 0)),
        out_shape=jax.ShapeDtypeStruct((batch, seq, D_MODEL), F32),
        scratch_shapes=scratch,
        compiler_params=pltpu.CompilerParams(
            dimension_semantics=("arbitrary", "arbitrary", "arbitrary"), vmem_limit_bytes=VMEM_LIMIT_BYTES),
        name="ffn",
    )(x, mod4, prm["ng_ffn"], prm["wup"], prm["wdn"], prm["fg"])


def _trunk(x, mod4, conv_in, delta_in, prm, sgw, sgb, *, mod_off, bg, p, pv, ffn_bb, ffn_tm, ffn_ffc, emit_v):
    convs, deltas, vrows = [], [], []
    for l in range(DEPTH):
        outs = _mixer(x, mod4, conv_in, delta_in, prm, sgw, sgb, layer=l, mod_off=mod_off,
                      bg=bg, p=p, pv=pv, emit_v=emit_v)
        convs.append(outs[1])
        deltas.append(outs[2])
        if emit_v:
            vrows.append(outs[3])
        x = _ffn(outs[0], mod4, prm, layer=l, mod_off=mod_off, bb=ffn_bb, tm=ffn_tm, ffc=ffn_ffc,
                 final_norm=(l == DEPTH - 1))
    return x, jnp.stack(convs), jnp.stack(deltas), (jnp.stack(vrows) if emit_v else None)


def kernel(x_prompt, x_sample, c_prompt, c_sample, state_conv, state_delta, ada_w, ada_b, norm_mix_g,
           norm_ffn_g, w_in, sgu_norm_g, sgu_w, sgu_b, conv_w, dt_bias, a_log, dn_norm_g, w_out, w_up,
           w_down, final_norm_g):
    nb, seq, _ = x_prompt.shape
    ns, dec_seq, _ = x_sample.shape
    sgu_chunk = sgu_w.shape[-1]

    head_lanes = ((0, 0), (G_LANE0, AB_LANES - 2 * DN_HEADS))
    prm = dict(
        ng_mix=norm_mix_g[:, None, :], ng_ffn=norm_ffn_g[:, None, :], sgng=sgu_norm_g[:, None, :],
        win=jnp.pad(w_in, ((0, 0), (0, 0), (0, IN_PAD - IN_WIDTH))).astype(BF16),
        cw=conv_w, dtb=jnp.pad(dt_bias, head_lanes)[:, None, :],
        alog=jnp.pad(a_log, head_lanes)[:, None, :], dng=dn_norm_g[:, None, :],
        wout=w_out.astype(BF16), wup=w_up.astype(BF16), wdn=w_down.astype(BF16), fg=final_norm_g[None, :])
    sgb_p = jnp.broadcast_to(sgu_b[:, :, :, None], (DEPTH, SGU_GROUPS, sgu_chunk, GROUP_DIM))
    pad = CHUNK - dec_seq
    sgw_s = jnp.pad(sgu_w[:, :, :dec_seq, :dec_seq], ((0, 0), (0, 0), (0, pad), (0, pad)))
    sgb_s = jnp.pad(sgb_p[:, :, :dec_seq, :], ((0, 0), (0, 0), (0, pad), (0, 0)))
    conv_in = jnp.pad(state_conv, ((0, 0), (0, 0), (CONV_PAD - (CONV_W - 1), 0), (0, 0)))

    mod_all = _modulation(jnp.concatenate([c_prompt, c_sample], axis=0), ada_w, ada_b)
    mod4 = mod_all.reshape(DEPTH, nb + ns, 6, D_MODEL)

    y_prompt, prompt_conv, prompt_delta, _ = _trunk(
        x_prompt, mod4, None, None, prm, sgu_w, sgb_p,
        mod_off=0, bg=4, p=sgu_chunk, pv=sgu_chunk, ffn_bb=1, ffn_tm=512, ffn_ffc=1, emit_v=False)

    y_sample, sample_conv, sample_delta, sample_v = _trunk(
        x_sample, mod4, conv_in, state_delta, prm, sgw_s, sgb_s,
        mod_off=nb, bg=ns, p=CHUNK, pv=dec_seq, ffn_bb=ns, ffn_tm=dec_seq, ffn_ffc=4, emit_v=True)

    return (y_prompt, y_sample, prompt_conv, prompt_delta, sample_conv, sample_delta, sample_v)
```

```python
import functools

import jax
import jax.numpy as jnp
from jax import lax
from jax.experimental import pallas as pl
from jax.experimental.pallas import tpu as pltpu

F32 = jnp.float32
BF16 = jnp.bfloat16

D_MODEL = 1024
DEPTH = 2
CHUNK = 64
CHUNK_LOG2 = 6
SGU_WIDTH = 512
SGU_GROUPS = 4
GROUP_DIM = 128
DN_WIDTH = 512
DN_HEADS = 4
HEAD_DIM = 128
CONV_W = 4
CONV_CH = 3 * DN_WIDTH
D_FF = 4 * D_MODEL
EPS = 1e-6

O_UV = 0
O_QKV = 2 * SGU_WIDTH
O_Z = O_QKV + CONV_CH
O_AB = O_Z + DN_WIDTH
AB_LANES = 128
IN_WIDTH = O_AB + 2 * DN_HEADS
IN_PAD = O_AB + AB_LANES
G_LANE0 = DN_HEADS
CONV_PAD = 8

VMEM_LIMIT_BYTES = 56 * 1024 * 1024
MXU_COLS = 256


def _sigmoid(x):
    return 0.5 * (1.0 + jnp.tanh(0.5 * x))


def _silu(x):
    h = 0.5 * x
    return h + h * jnp.tanh(h)


def _softplus(x):
    return jnp.maximum(x, 0.0) + jnp.log(1.0 + jnp.exp(-jnp.abs(x)))


def _gelu_tanh(x):
    c = 0.7978845608028654
    h = 0.5 * x
    return h + h * jnp.tanh(x * (c + (c * 0.044715) * (x * x)))


def _dot(a, b):
    return jnp.dot(a, b, preferred_element_type=F32)


def _dot_nt(a, b):
    return lax.dot_general(a, b, (((1,), (1,)), ((), ())), preferred_element_type=F32)


def _split3(x):
    hi = x.astype(BF16)
    r = x - hi.astype(F32)
    mid = r.astype(BF16)
    lo = (r - mid.astype(F32)).astype(BF16)
    return hi, mid, lo


def _blockdiag_lanes128(y):
    yb = y.astype(BF16)
    z = jnp.zeros((y.shape[0], HEAD_DIM), BF16)
    top = jnp.concatenate([yb[:, :HEAD_DIM], z], axis=1)
    bot = jnp.concatenate([z, yb[:, HEAD_DIM:]], axis=1)
    return jnp.concatenate([top, bot], axis=0)


def _interleave(groups):
    keyed = [((i + 0.5) / len(g), gi, i, f) for gi, g in enumerate(groups) for i, f in enumerate(g)]
    return [f for _, _, _, f in sorted(keyed, key=lambda e: e[:3])]


def _layer_spec(shape, layer):
    return pl.BlockSpec((None,) + tuple(shape), lambda i, j: (layer,) + (0,) * len(shape))


MOD_BLOCK = 2048


def _mod_kernel(c_ref, w_ref, b_ref, o_ref):
    cs = _silu(c_ref[...]).astype(BF16)
    o_ref[...] = _dot(cs, w_ref[...].astype(BF16)) + b_ref[...]


def _modulation(c_all, ada_w, ada_b):
    n = c_all.shape[0]
    width = ada_w.shape[2]
    return pl.pallas_call(
        _mod_kernel,
        grid=(DEPTH, width // MOD_BLOCK),
        in_specs=[
            pl.BlockSpec((n, D_MODEL), lambda l, j: (0, 0)),
            pl.BlockSpec((None, D_MODEL, MOD_BLOCK), lambda l, j: (l, 0, j)),
            pl.BlockSpec((None, 1, MOD_BLOCK), lambda l, j: (l, 0, j)),
        ],
        out_specs=pl.BlockSpec((None, n, MOD_BLOCK), lambda l, j: (l, 0, j)),
        out_shape=jax.ShapeDtypeStruct((DEPTH, n, width), F32),
        compiler_params=pltpu.CompilerParams(
            dimension_semantics=("arbitrary", "arbitrary"), vmem_limit_bytes=VMEM_LIMIT_BYTES),
        name="adaln_modulation",
    )(c_all, ada_w, ada_b.reshape(DEPTH, 1, width))


W_IN_ROWS = 256


def _cast_pad_kernel(w_ref, o_ref):
    o_ref[:, 0:O_AB] = w_ref[:, 0:O_AB].astype(BF16)
    logits = w_ref[:, O_AB:IN_WIDTH]
    zeros = jnp.zeros((logits.shape[0], IN_PAD - IN_WIDTH), F32)
    o_ref[:, O_AB:IN_PAD] = jnp.concatenate([logits, zeros], axis=1).astype(BF16)


def _cast_pad_w_in(w_in):
    return pl.pallas_call(
        _cast_pad_kernel,
        grid=(DEPTH, D_MODEL // W_IN_ROWS),
        in_specs=[pl.BlockSpec((None, W_IN_ROWS, IN_WIDTH), lambda l, i: (l, i, 0))],
        out_specs=pl.BlockSpec((None, W_IN_ROWS, IN_PAD), lambda l, i: (l, i, 0)),
        out_shape=jax.ShapeDtypeStruct((DEPTH, D_MODEL, IN_PAD), BF16),
        compiler_params=pltpu.CompilerParams(
            dimension_semantics=("arbitrary", "arbitrary"), vmem_limit_bytes=VMEM_LIMIT_BYTES),
        name="cast_pad_w_in",
    )(w_in)


def _mixer_kernel(*refs, bg, p, pv, emit_v, zero_init):
    refs = list(refs)
    x_ref, mod_ref = refs[:2]
    refs = refs[2:]
    if not zero_init:
        conv0_ref, delta0_ref = refs[:2]
        refs = refs[2:]
    (ng_ref, win_ref, sgng_ref, sgw_ref, sgb_ref, cw_ref, dtb_ref, alog_ref, dng_ref, wout_ref,
     xo_ref, convo_ref, deltao_ref) = refs[:13]
    refs = refs[13:]
    if emit_v:
        vo_ref = refs[0]
        refs = refs[1:]
    ctail, cbuf, act, abl, zs, mixb = refs
    nchunk = p // CHUNK
    step = pl.program_id(1)

    @pl.when(step == 0)
    def _():
        if zero_init:
            ctail[...] = jnp.zeros(ctail.shape, F32)
            deltao_ref[...] = jnp.zeros(deltao_ref.shape, F32)
        else:
            ctail[...] = conv0_ref[...]
            deltao_ref[...] = delta0_ref[...]

    nhalf = 2 if bg % 2 == 0 else 1
    hbg = bg // nhalf
    hrows = hbg * p
    mod = mod_ref[...]
    cw = cw_ref[...]
    ri = lax.broadcasted_iota(jnp.int32, (p, p), 0)
    ci = lax.broadcasted_iota(jnp.int32, (p, p), 1)
    sg_mask = (ci >> CHUNK_LOG2) <= (ri >> CHUNK_LOG2)
    sg_w = [jnp.where(sg_mask, sgw_ref[g], 0.0).astype(BF16) for g in range(SGU_GROUPS)]
    state = [dict() for _ in range(nhalf)]
    pre = {}

    def stage_norm(h):
        st, b0 = state[h], h * hbg

        def run():
            x3 = x_ref[b0:b0 + hbg]
            ms = jnp.mean(x3 * x3, axis=-1, keepdims=True)
            gain = ng_ref[...] * (1.0 + mod[b0:b0 + hbg, 1:2, :])
            h3 = (x3 * lax.rsqrt(ms + EPS)) * gain + mod[b0:b0 + hbg, 0:1, :]
            if pv < p:
                h3 = jnp.concatenate([h3, jnp.zeros((hbg, p - pv, D_MODEL), F32)], axis=1)
            st["hb"] = h3.reshape(hrows, D_MODEL).astype(BF16)
        return [run]

    def stage_uv(h):
        st = state[h]
        st["uv"] = [None] * (O_QKV // MXU_COLS)

        def piece(n):
            def run():
                st["uv"][n] = _dot(st["hb"], win_ref[:, O_UV + n * MXU_COLS:O_UV + (n + 1) * MXU_COLS])
            return run
        return [piece(n) for n in range(O_QKV // MXU_COLS)]

    def stage_gate(h):
        st, b0 = state[h], h * hbg
        nblk = O_QKV // MXU_COLS

        def act_piece(n):
            def run():
                st["uv"][n] = _gelu_tanh(st["uv"][n])
            return run

        def vnorm():
            v = jnp.concatenate(st["uv"][nblk // 2:], axis=1)
            vms = jnp.mean(v * v, axis=-1, keepdims=True)
            vn = (v * lax.rsqrt(vms + EPS)) * sgng_ref[...]
            if emit_v:
                vo_ref[b0:b0 + hbg] = vn.reshape(hbg, p, SGU_WIDTH)[:, :pv, :]
            st["vnb"] = vn.astype(BF16)
            st["u"] = jnp.concatenate(st["uv"][:nblk // 2], axis=1)

        def gate_piece(g):
            def run():
                lo, hi = g * GROUP_DIM, (g + 1) * GROUP_DIM
                v_all = jnp.concatenate([st["vnb"][b * p:(b + 1) * p, lo:hi] for b in range(hbg)], axis=1)
                s_all = _dot(sg_w[g], v_all)
                for b in range(hbg):
                    s = s_all[:, b * GROUP_DIM:(b + 1) * GROUP_DIM] + sgb_ref[g]
                    mixb[(b0 + b) * p:(b0 + b + 1) * p, lo:hi] = (
                        st["u"][b * p:(b + 1) * p, lo:hi] * s).astype(BF16)
            return run
        return [act_piece(n) for n in range(nblk)] + [vnorm] + [gate_piece(g) for g in range(SGU_GROUPS)]

    def stage_qkv(h):
        st, b0 = state[h], h * hbg

        def piece(n):
            def run():
                if n == 0:
                    cbuf[b0:b0 + hbg, 0:CONV_PAD, :] = ctail[b0:b0 + hbg]
                lo, hi = n * MXU_COLS, (n + 1) * MXU_COLS
                cbuf[b0:b0 + hbg, CONV_PAD:CONV_PAD + p, lo:hi] = _dot(
                    st["hb"], win_ref[:, O_QKV + lo:O_QKV + hi]).reshape(hbg, p, MXU_COLS)
            return run
        return [piece(n) for n in range(CONV_CH // MXU_COLS)]

    def stage_conv(h):
        b0 = h * hbg
        r0 = b0 * p

        def piece(n):
            def run():
                lo, hi = n * MXU_COLS, (n + 1) * MXU_COLS
                y = cbuf[b0:b0 + hbg, CONV_PAD:CONV_PAD + p, lo:hi] * cw[CONV_W - 1:CONV_W, lo:hi]
                for i in range(1, CONV_W):
                    y = y + (cbuf[b0:b0 + hbg, CONV_PAD - i:CONV_PAD - i + p, lo:hi]
                             * cw[CONV_W - 1 - i:CONV_W - i, lo:hi])
                a = _silu(y).reshape(hrows, MXU_COLS)
                for lo_h in range(lo, hi, HEAD_DIM):
                    t = a[:, lo_h - lo:lo_h - lo + HEAD_DIM]
                    if lo_h < 2 * DN_WIDTH:
                        scale = HEAD_DIM ** -0.5 if lo_h < DN_WIDTH else 1.0
                        t = t * (lax.rsqrt(jnp.sum(t * t, axis=-1, keepdims=True) + EPS) * scale)
                    act[r0:r0 + hrows, lo_h:lo_h + HEAD_DIM] = t
            return run

        def tail():
            ctail[b0:b0 + hbg] = cbuf[b0:b0 + hbg, pv:pv + CONV_PAD, :]
            convo_ref[b0:b0 + hbg] = cbuf[b0:b0 + hbg, pv + CONV_PAD - (CONV_W - 1):pv + CONV_PAD, :]
        return [piece(n) for n in range(CONV_CH // MXU_COLS)] + [tail]

    def stage_zab(h):
        st = state[h]
        r0 = h * hrows

        def z_piece(n):
            def run():
                lo, hi = n * MXU_COLS, (n + 1) * MXU_COLS
                zs[r0:r0 + hrows, lo:hi] = _silu(_dot(st["hb"], win_ref[:, O_Z + lo:O_Z + hi]))
            return run

        def ab_piece():
            abl[r0:r0 + hrows, :] = _dot(st["hb"], win_ref[:, O_AB:IN_PAD])
        return [z_piece(n) for n in range(DN_WIDTH // MXU_COLS)] + [ab_piece]

    ii = lax.broadcasted_iota(jnp.int32, (CHUNK, 2 * HEAD_DIM), 0)
    lane4 = lax.broadcasted_iota(jnp.int32, (CHUNK, 2 * HEAD_DIM), 1)
    jj = lane4 & (CHUNK - 1)
    incl = ii >= jj
    strict = ii > jj
    eye4 = jnp.where(ii == jj, 1.0, 0.0).astype(F32)
    head_of_lane = lane4 >> CHUNK_LOG2
    left2 = lax.broadcasted_iota(jnp.int32, (CHUNK, HEAD_DIM), 1) < CHUNK
    kk = lax.broadcasted_iota(jnp.int32, (CHUNK, 3 * CHUNK), 1) & (CHUNK - 1)
    rr = lax.broadcasted_iota(jnp.int32, (CHUNK, 3 * CHUNK), 0)
    ltri3 = jnp.where(kk <= rr, 1.0, 0.0).astype(BF16)
    level_masks = []
    for k in range(CHUNK_LOG2):
        level_masks.append(jnp.where(((ii ^ jj) >> k) == 1, (ii >> k) & 1, 0) == 1)
    dtb = dtb_ref[...]
    neg_decay = -jnp.exp(alog_ref[...])
    dng = dng_ref[...]
    pos = lax.broadcasted_iota(jnp.int32, (CHUNK, AB_LANES), 0)

    def lanes(t, lane0):
        return jnp.concatenate(
            [jnp.broadcast_to(t[:, lane0 + hh:lane0 + hh + 1], (CHUNK, HEAD_DIM)) for hh in range(DN_HEADS)], axis=1)

    def pack4(t):
        c0 = jnp.where(left2, t[:, 0:128], t[:, 128:256])
        c1 = jnp.where(left2, t[:, 256:384], t[:, 384:512])
        return jnp.concatenate([c0, c1], axis=1)

    head_sel = [jnp.where(head_of_lane == hh, 1.0, 0.0).astype(BF16) for hh in range(DN_HEADS)]
    level_sel = [[jnp.where(head_of_lane == hh, jnp.where(level_masks[k], 1.0, 0.0), 0.0).astype(BF16)
                  for hh in range(DN_HEADS)] for k in range(CHUNK_LOG2)]

    def blockdiag4(yb, sel):
        return jnp.concatenate([yb * sel[hh] for hh in range(DN_HEADS)], axis=0)

    def stage_pre(h):
        def piece(b, c):
            def run():
                r0 = b * p + c * CHUNK
                ab = abl[r0:r0 + CHUNK, :]
                beta_c = _sigmoid(ab)
                g_c = neg_decay * _softplus(ab + dtb)
                if pv < p:
                    valid = (pos + c * CHUNK) < pv
                    beta_c = jnp.where(valid, beta_c, 0.0)
                    g_c = jnp.where(valid, g_c, 0.0)
                gc_c = _dot(ltri3, jnp.concatenate(_split3(g_c), axis=0))
                glast_c = gc_c[CHUNK - 1:CHUNK, :]
                beta = lanes(beta_c, 0)
                egc = lanes(jnp.exp(gc_c), G_LANE0)
                ekg = lanes(jnp.exp(glast_c - gc_c), G_LANE0)
                gl = jnp.concatenate([jnp.broadcast_to(jnp.exp(glast_c)[:, G_LANE0 + hh:G_LANE0 + hh + 1],
                                                       (1, HEAD_DIM)) for hh in range(DN_HEADS)], axis=1)
                gc_t = gc_c.T
                grow4 = jnp.broadcast_to(
                    jnp.concatenate([gc_t[G_LANE0 + hh:G_LANE0 + hh + 1, :] for hh in range(DN_HEADS)], axis=1),
                    (CHUNK, DN_HEADS * CHUNK))
                diff4 = pack4(lanes(gc_c, G_LANE0)) - grow4
                dec_incl = jnp.exp(jnp.where(incl, diff4, -jnp.inf))
                dec_strict = jnp.where(strict, dec_incl, 0.0)
                qn = act[r0:r0 + CHUNK, 0:DN_WIDTH]
                kn = act[r0:r0 + CHUNK, DN_WIDTH:2 * DN_WIDTH]
                vv = act[r0:r0 + CHUNK, 2 * DN_WIDTH:3 * DN_WIDTH]
                qb = qn.astype(BF16)
                kb = kn.astype(BF16)
                qk_cols, kk_cols = [], []
                for pr in range(DN_HEADS // 2):
                    lo, hi = pr * 2 * HEAD_DIM, (pr + 1) * 2 * HEAD_DIM
                    lhs = jnp.concatenate([qb[:, lo:hi], kb[:, lo:hi]], axis=0)
                    r = _dot_nt(lhs, _blockdiag_lanes128(kn[:, lo:hi]))
                    qk_cols.append(r[:CHUNK])
                    kk_cols.append(r[CHUNK:])
                attn4 = jnp.concatenate(qk_cols, axis=1) * dec_incl
                a4 = jnp.concatenate(kk_cols, axis=1) * (pack4(beta) * dec_strict)
                pre[(b, c)] = dict(a4=a4, attn4=attn4.astype(BF16), beta=beta, vv=vv,
                                   kgc=(kn * egc).astype(BF16), qg=(qn * egc).astype(BF16),
                                   kg=kn * ekg, gl=gl)
            return run
        return [piece(b, c) for b in range(h * hbg, (h + 1) * hbg) for c in range(nchunk)]

    stages = [stage_norm, stage_uv, stage_gate, stage_qkv, stage_conv, stage_zab, stage_pre]
    for t in range(len(stages) + nhalf - 1):
        active = [stages[t - h](h) for h in range(nhalf) if 0 <= t - h < len(stages)]
        for run in _interleave(active):
            run()

    probs = [(b, c) for b in range(bg) for c in range(nchunk)]
    a4 = {key: pre[key]["a4"] for key in probs}
    a4b = {key: a4[key].astype(BF16) for key in probs}
    dinv = {key: eye4 - jnp.where(level_masks[0], a4[key], 0.0) for key in probs}
    for k in range(1, CHUNK_LOG2):
        db = {key: dinv[key].astype(BF16) for key in probs}
        p1 = {key: _dot(db[key], blockdiag4(a4b[key], level_sel[k])) for key in probs}
        p2 = {key: _dot(p1[key].astype(BF16), blockdiag4(db[key], head_sel)) for key in probs}
        dinv = {key: dinv[key] - p2[key] for key in probs}

    pairs = [(b, pr) for b in range(bg) for pr in range(DN_HEADS // 2)]
    for c in range(nchunk):
        s_pair, r, v_new, yy = {}, {}, {}, {}
        for (b, pr) in pairs:
            pc = pre[(b, c)]
            lo, hi = pr * 2 * HEAD_DIM, (pr + 1) * 2 * HEAD_DIM
            s_pair[b, pr] = jnp.concatenate([deltao_ref[b, 2 * pr], deltao_ref[b, 2 * pr + 1]], axis=1)
            lhs = jnp.concatenate([pc["kgc"][:, lo:hi], pc["qg"][:, lo:hi]], axis=0)
            r[b, pr] = _dot(lhs, _blockdiag_lanes128(s_pair[b, pr]))
        for (b, pr) in pairs:
            pc = pre[(b, c)]
            lo, hi = pr * 2 * HEAD_DIM, (pr + 1) * 2 * HEAD_DIM
            xr = pc["beta"][:, lo:hi] * (pc["vv"][:, lo:hi] - r[b, pr][:CHUNK])
            t_pair = dinv[(b, c)][:, pr * HEAD_DIM:(pr + 1) * HEAD_DIM].astype(BF16)
            v_new[b, pr] = _dot(t_pair, _blockdiag_lanes128(xr))
        for (b, pr) in pairs:
            pc = pre[(b, c)]
            lo, hi = pr * 2 * HEAD_DIM, (pr + 1) * 2 * HEAD_DIM
            kg_pair = jnp.concatenate([pc["kg"][:, lo:lo + HEAD_DIM], pc["kg"][:, lo + HEAD_DIM:hi]],
                                      axis=0)
            lhs2 = jnp.concatenate([pc["attn4"][:, pr * HEAD_DIM:(pr + 1) * HEAD_DIM],
                                    kg_pair.T.astype(BF16)], axis=0)
            yy[b, pr] = _dot(lhs2, _blockdiag_lanes128(v_new[b, pr]))
        for (b, pr) in pairs:
            lo, hi = pr * 2 * HEAD_DIM, (pr + 1) * 2 * HEAD_DIM
            s_new = s_pair[b, pr] * pre[(b, c)]["gl"][:, lo:hi] + yy[b, pr][CHUNK:]
            deltao_ref[b, 2 * pr] = s_new[:, :HEAD_DIM]
            deltao_ref[b, 2 * pr + 1] = s_new[:, HEAD_DIM:]
        for b in range(bg):
            r0 = b * p + c * CHUNK
            for hh in range(DN_HEADS):
                lo = (hh % 2) * HEAD_DIM
                o_h = r[b, hh // 2][CHUNK:, lo:lo + HEAD_DIM] + yy[b, hh // 2][:CHUNK, lo:lo + HEAD_DIM]
                oms = jnp.mean(o_h * o_h, axis=-1, keepdims=True)
                gated = (o_h * lax.rsqrt(oms + EPS)) * dng * zs[r0:r0 + CHUNK, hh * HEAD_DIM:(hh + 1) * HEAD_DIM]
                mixb[r0:r0 + CHUNK, SGU_WIDTH + hh * HEAD_DIM:SGU_WIDTH + (hh + 1) * HEAD_DIM] = gated.astype(BF16)

    out = _dot(mixb[...], wout_ref[...]).reshape(bg, p, D_MODEL)
    xo_ref[...] = x_ref[...] + mod[:, 2:3, :] * out[:, :pv, :]


def _mixer(x, mod4, conv_in, delta_in, prm, sgw, sgb, *, layer, mod_off, bg, p, pv, emit_v):
    batch, seq, _ = x.shape
    zero_init = conv_in is None
    grid = (batch // bg, seq // pv)
    rows = bg * p
    mod_blk = mod_off // bg
    in_specs = [
        pl.BlockSpec((bg, pv, D_MODEL), lambda i, j: (i, j, 0)),
        pl.BlockSpec((None, bg, 6, D_MODEL), lambda i, j: (layer, i + mod_blk, 0, 0)),
    ]
    args = [x, mod4]
    if not zero_init:
        in_specs += [
            pl.BlockSpec((None, bg, CONV_PAD, CONV_CH), lambda i, j: (layer, i, 0, 0)),
            pl.BlockSpec((None, bg, DN_HEADS, HEAD_DIM, HEAD_DIM), lambda i, j: (layer, i, 0, 0, 0)),
        ]
        args += [conv_in, delta_in]
    in_specs += [
        _layer_spec((1, D_MODEL), layer), _layer_spec((D_MODEL, IN_PAD), layer),
        _layer_spec((1, SGU_WIDTH), layer), _layer_spec((SGU_GROUPS, p, p), layer),
        _layer_spec((SGU_GROUPS, p, GROUP_DIM), layer), _layer_spec((CONV_W, CONV_CH), layer),
        _layer_spec((1, AB_LANES), layer), _layer_spec((1, AB_LANES), layer),
        _layer_spec((1, HEAD_DIM), layer), _layer_spec((D_MODEL, D_MODEL), layer),
    ]
    args += [prm["ng_mix"], prm["win"], prm["sgng"], sgw, sgb, prm["cw"], prm["dtb"], prm["alog"],
             prm["dng"], prm["wout"]]
    out_specs = [
        pl.BlockSpec((bg, pv, D_MODEL), lambda i, j: (i, j, 0)),
        pl.BlockSpec((bg, CONV_W - 1, CONV_CH), lambda i, j: (i, 0, 0)),
        pl.BlockSpec((bg, DN_HEADS, HEAD_DIM, HEAD_DIM), lambda i, j: (i, 0, 0, 0)),
    ]
    out_shape = [
        jax.ShapeDtypeStruct((batch, seq, D_MODEL), F32),
        jax.ShapeDtypeStruct((batch, CONV_W - 1, CONV_CH), F32),
        jax.ShapeDtypeStruct((batch, DN_HEADS, HEAD_DIM, HEAD_DIM), F32),
    ]
    if emit_v:
        out_specs.append(pl.BlockSpec((bg, pv, SGU_WIDTH), lambda i, j: (i, j, 0)))
        out_shape.append(jax.ShapeDtypeStruct((batch, seq, SGU_WIDTH), F32))
    scratch = [
        pltpu.VMEM((bg, CONV_PAD, CONV_CH), F32),
        pltpu.VMEM((bg, CONV_PAD + p, CONV_CH), F32),
        pltpu.VMEM((rows, CONV_CH), F32),
        pltpu.VMEM((rows, AB_LANES), F32),
        pltpu.VMEM((rows, DN_WIDTH), F32),
        pltpu.VMEM((rows, D_MODEL), BF16),
    ]
    return pl.pallas_call(
        functools.partial(_mixer_kernel, bg=bg, p=p, pv=pv, emit_v=emit_v, zero_init=zero_init),
        grid=grid, in_specs=in_specs, out_specs=out_specs, out_shape=out_shape,
        scratch_shapes=scratch,
        compiler_params=pltpu.CompilerParams(
            dimension_semantics=("arbitrary", "arbitrary"), vmem_limit_bytes=VMEM_LIMIT_BYTES),
        name="mixer",
    )(*args)


def _ffn_kernel(x_ref, mod_ref, ng_ref, wup_ref, wdn_ref, fg_ref, o_ref, *scratch, bb, tm, ffc, final_norm):
    mod = mod_ref[...]

    def normed():
        x3 = x_ref[...]
        ms = jnp.mean(x3 * x3, axis=-1, keepdims=True)
        gain = ng_ref[...] * (1.0 + mod[:, 4:5, :])
        h3 = (x3 * lax.rsqrt(ms + EPS)) * gain + mod[:, 3:4, :]
        return h3.reshape(bb * tm, D_MODEL).astype(BF16)

    def hidden(hb):
        a = jnp.maximum(_dot(hb, wup_ref[...]), 0.0)
        return _dot((a * a).astype(BF16), wdn_ref[...])

    def finish(down):
        y = x_ref[...] + mod[:, 5:6, :] * down.reshape(bb, tm, D_MODEL)
        if final_norm:
            yms = jnp.mean(y * y, axis=-1, keepdims=True)
            y = (y * lax.rsqrt(yms + EPS)) * fg_ref[...]
        o_ref[...] = y

    if ffc == 1:
        finish(hidden(normed()))
        return
    hbuf, acc = scratch
    c = pl.program_id(2)

    @pl.when(c == 0)
    def _():
        hbuf[...] = normed()
        acc[...] = jnp.zeros(acc.shape, F32)

    acc[...] += hidden(hbuf[...])

    @pl.when(c == ffc - 1)
    def _():
        finish(acc[...])


def _ffn(x, mod4, prm, *, layer, mod_off, bb, tm, ffc, final_norm):
    batch, seq, _ = x.shape
    mod_blk = mod_off // bb
    scratch = [] if ffc == 1 else [pltpu.VMEM((bb * tm, D_MODEL), BF16), pltpu.VMEM((bb * tm, D_MODEL), F32)]
    return pl.pallas_call(
        functools.partial(_ffn_kernel, bb=bb, tm=tm, ffc=ffc, final_norm=final_norm),
        grid=(batch // bb, seq // tm, ffc),
        in_specs=[
            pl.BlockSpec((bb, tm, D_MODEL), lambda i, j, c: (i, j, 0)),
            pl.BlockSpec((None, bb, 6, D_MODEL), lambda i, j, c: (layer, i + mod_blk, 0, 0)),
            pl.BlockSpec((None, 1, D_MODEL), lambda i, j, c: (layer, 0, 0)),
            pl.BlockSpec((None, D_MODEL, D_FF // ffc), lambda i, j, c: (layer, 0, c)),
            pl.BlockSpec((None, D_FF // ffc, D_MODEL), lambda i, j, c: (layer, c, 0)),
            pl.BlockSpec((1, D_MODEL), lambda i, j, c: (0, 0)),
        ],
        out_specs=pl.BlockSpec((bb, tm, D_MODEL), lambda i, j, c: (i, j, 0)),
        out_shape=jax.ShapeDtypeStruct((batch, seq, D_MODEL), F32),
        scratch_shapes=scratch,
        compiler_params=pltpu.CompilerParams(
            dimension_semantics=("arbitrary", "arbitrary", "arbitrary"), vmem_limit_bytes=VMEM_LIMIT_BYTES),
        name="ffn",
    )(x, mod4, prm["ng_ffn"], prm["wup"], prm["wdn"], prm["fg"])


def _trunk(x, mod4, conv_in, delta_in, prm, sgw, sgb, *, mod_off, bg, p, pv, ffn_bb, ffn_tm, ffn_ffc, emit_v):
    convs, deltas, vrows = [], [], []
    for l in range(DEPTH):
        outs = _mixer(x, mod4, conv_in, delta_in, prm, sgw, sgb, layer=l, mod_off=mod_off,
                      bg=bg, p=p, pv=pv, emit_v=emit_v)
        convs.append(outs[1])
        deltas.append(outs[2])
        if emit_v:
            vrows.append(outs[3])
        x = _ffn(outs[0], mod4, prm, layer=l, mod_off=mod_off, bb=ffn_bb, tm=ffn_tm, ffc=ffn_ffc,
                 final_norm=(l == DEPTH - 1))
    return x, jnp.stack(convs), jnp.stack(deltas), (jnp.stack(vrows) if emit_v else None)


def kernel(x_prompt, x_sample, c_prompt, c_sample, state_conv, state_delta, ada_w, ada_b, norm_mix_g,
           norm_ffn_g, w_in, sgu_norm_g, sgu_w, sgu_b, conv_w, dt_bias, a_log, dn_norm_g, w_out, w_up,
           w_down, final_norm_g):
    nb, seq, _ = x_prompt.shape
    ns, dec_seq, _ = x_sample.shape
    sgu_chunk = sgu_w.shape[-1]

    head_lanes = ((0, 0), (G_LANE0, AB_LANES - 2 * DN_HEADS))
    prm = dict(
        ng_mix=norm_mix_g[:, None, :], ng_ffn=norm_ffn_g[:, None, :], sgng=sgu_norm_g[:, None, :],
        win=_cast_pad_w_in(w_in),
        cw=conv_w, dtb=jnp.pad(dt_bias, head_lanes)[:, None, :],
        alog=jnp.pad(a_log, head_lanes)[:, None, :], dng=dn_norm_g[:, None, :],
        wout=w_out.astype(BF16), wup=w_up.astype(BF16), wdn=w_down.astype(BF16), fg=final_norm_g[None, :])
    sgb_p = jnp.broadcast_to(sgu_b[:, :, :, None], (DEPTH, SGU_GROUPS, sgu_chunk, GROUP_DIM))
    pad = CHUNK - dec_seq
    sgw_s = jnp.pad(sgu_w[:, :, :dec_seq, :dec_seq], ((0, 0), (0, 0), (0, pad), (0, pad)))
    sgb_s = jnp.pad(sgb_p[:, :, :dec_seq, :], ((0, 0), (0, 0), (0, pad), (0, 0)))
    conv_in = jnp.pad(state_conv, ((0, 0), (0, 0), (CONV_PAD - (CONV_W - 1), 0), (0, 0)))

    mod_all = _modulation(jnp.concatenate([c_prompt, c_sample], axis=0), ada_w, ada_b)
    mod4 = mod_all.reshape(DEPTH, nb + ns, 6, D_MODEL)

    y_prompt, prompt_conv, prompt_delta, _ = _trunk(
        x_prompt, mod4, None, None, prm, sgu_w, sgb_p,
        mod_off=0, bg=nb, p=sgu_chunk, pv=sgu_chunk, ffn_bb=1, ffn_tm=512, ffn_ffc=1, emit_v=False)

    y_sample, sample_conv, sample_delta, sample_v = _trunk(
        x_sample, mod4, conv_in, state_delta, prm, sgw_s, sgb_s,
        mod_off=nb, bg=ns, p=CHUNK, pv=dec_seq, ffn_bb=ns, ffn_tm=dec_seq, ffn_ffc=4, emit_v=True)

    return (y_prompt, y_sample, prompt_conv, prompt_delta, sample_conv, sample_delta, sample_v)
```
